```python
import jax, jax.numpy as jnp
from jax import lax
import numpy as np

D_MODEL = 1024
BATCH = 8
SEQ = 4096
DEPTH = 2

HEAD_DIM = 64
ROPE_THETA = 10000.0
NORM_EPS = 1e-6
BLOCK = 128

A_HEADS = 6
IDX_HEADS = 8
IDX_DIM = 64
TOPK_MAX = 256

B_GROUPS = ((128, 1), (512, 4), (2048, 16))
B_HEADS_PER_GROUP = 4
B_HEADS = B_HEADS_PER_GROUP * len(B_GROUPS)

C_HEADS = 8
C_KV_HEADS = 2
C_WINDOW = 128

N_BRANCHES = 3
D_FF = 4 * D_MODEL

IN_WIDTHS = (A_HEADS * HEAD_DIM, HEAD_DIM, HEAD_DIM,
             IDX_HEADS * IDX_DIM, IDX_DIM, IDX_HEADS,
             B_HEADS * HEAD_DIM, B_HEADS * HEAD_DIM, B_HEADS * HEAD_DIM,
             C_HEADS * HEAD_DIM, C_KV_HEADS * HEAD_DIM, C_KV_HEADS * HEAD_DIM,
             N_BRANCHES * D_MODEL)
IN_WIDTH = sum(IN_WIDTHS)

kernel_name = "hybrid_gated_dsa_dilated_sinkswa"


def rmsnorm(x, g):
    xf = x.astype(jnp.float32)
    y = xf * lax.rsqrt(jnp.mean(xf * xf, axis=-1, keepdims=True) + NORM_EPS)
    return (y * g.astype(jnp.float32)).astype(x.dtype)


def rope(x, pos):
    half = x.shape[-1] // 2
    inv = ROPE_THETA ** (-jnp.arange(half, dtype=jnp.float32) / half)
    ang = pos.astype(jnp.float32)[..., None] * inv
    cos = jnp.cos(ang)[:, :, None, :].astype(x.dtype)
    sin = jnp.sin(ang)[:, :, None, :].astype(x.dtype)
    x1, x2 = x[..., :half], x[..., half:]
    return jnp.concatenate([x1 * cos - x2 * sin, x2 * cos + x1 * sin], axis=-1)


def split_columns(z):
    parts, off = [], 0
    for w in IN_WIDTHS:
        parts.append(z[..., off:off + w])
        off += w
    return parts


def over_query_blocks(fn, seq_len):
    starts = jnp.arange(seq_len // BLOCK, dtype=jnp.int32) * BLOCK
    out = lax.map(fn, starts)
    out = jnp.moveaxis(out, 0, 1)
    return out.reshape(out.shape[0], seq_len, *out.shape[3:])


def dsa_attention(q, k, v, iq, ik, iw):
    L = q.shape[1]
    topk = min(TOPK_MAX, L // 4)
    scale = HEAD_DIM ** -0.5
    idx_scale = IDX_DIM ** -0.5
    s_all = jnp.arange(L)

    def block(t0):
        t = t0 + jnp.arange(BLOCK)
        qb = lax.dynamic_slice_in_dim(q, t0, BLOCK, 1)
        iqb = lax.dynamic_slice_in_dim(iq, t0, BLOCK, 1)
        iwb = lax.dynamic_slice_in_dim(iw, t0, BLOCK, 1).astype(jnp.float32)
        rel = jax.nn.relu(jnp.einsum('bqhd,bsd->bqhs', iqb, ik).astype(jnp.float32) * idx_scale)
        score = jnp.einsum('bqh,bqhs->bqs', iwb, rel)
        causal = s_all[None, :] <= t[:, None]
        score = jnp.where(causal[None], score, -jnp.inf)
        _, idx = lax.top_k(score, topk)
        ks = jax.vmap(lambda kk, ii: kk[ii])(k, idx)
        vs = jax.vmap(lambda vv, ii: vv[ii])(v, idx)
        valid = idx <= t[None, :, None]
        logits = jnp.einsum('bqhd,bqkd->bqhk', qb, ks).astype(jnp.float32) * scale
        logits = jnp.where(valid[:, :, None, :], logits, -jnp.inf)
        p = jax.nn.softmax(logits, axis=-1)
        return jnp.einsum('bqhk,bqkd->bqhd', p.astype(vs.dtype), vs)

    return over_query_blocks(block, L)


def dilated_attention(q, k, v):
    L = q.shape[1]
    scale = HEAD_DIM ** -0.5
    hg = B_HEADS_PER_GROUP
    qs = [q[:, :, g * hg:(g + 1) * hg] for g in range(len(B_GROUPS))]
    kss = [k[:, :, g * hg:(g + 1) * hg] for g in range(len(B_GROUPS))]
    vss = [v[:, :, g * hg:(g + 1) * hg] for g in range(len(B_GROUPS))]

    def block(t0):
        t = t0 + jnp.arange(BLOCK)
        outs, lses = [], []
        for g, (window, dil) in enumerate(B_GROUPS):
            qb = lax.dynamic_slice_in_dim(qs[g], t0, BLOCK, 1)
            pos = t[:, None] - dil * jnp.arange(window // dil + 1)[None, :]
            valid = pos >= 0
            pos_c = jnp.maximum(pos, 0)
            kg = kss[g][:, pos_c]
            vg = vss[g][:, pos_c]
            logits = jnp.einsum('bqhd,bqkhd->bqhk', qb, kg).astype(jnp.float32) * scale
            logits = jnp.where(valid[None, :, None, :], logits, -jnp.inf)
            m = jnp.max(logits, axis=-1, keepdims=True)
            e = jnp.exp(logits - m)
            den = jnp.sum(e, axis=-1)
            o = jnp.einsum('bqhk,bqkhd->bqhd', (e / den[..., None]).astype(vg.dtype), vg)
            outs.append(o)
            lses.append(m[..., 0] + jnp.log(den))
        alpha = jax.nn.softmax(jnp.stack(lses, 0), axis=0)
        return jnp.einsum('gbqh,gbqhd->bqhd', alpha.astype(outs[0].dtype), jnp.stack(outs, 0))

    return over_query_blocks(block, L)


def sink_window_attention(q, k, v, sinks):
    Bsz, L = q.shape[0], q.shape[1]
    grp = C_HEADS // C_KV_HEADS
    scale = HEAD_DIM ** -0.5
    pad = ((0, 0), (BLOCK, 0), (0, 0), (0, 0))
    kp = jnp.pad(k, pad)
    vp = jnp.pad(v, pad)
    qg = q.reshape(Bsz, L, C_KV_HEADS, grp, HEAD_DIM)
    sink = sinks.astype(jnp.float32).reshape(C_KV_HEADS, grp)[:, :, None, None]

    def block(t0):
        t = t0 + jnp.arange(BLOCK)
        s = t0 - BLOCK + jnp.arange(2 * BLOCK)
        qb = lax.dynamic_slice_in_dim(qg, t0, BLOCK, 1)
        kb = lax.dynamic_slice_in_dim(kp, t0, 2 * BLOCK, 1)
        vb = lax.dynamic_slice_in_dim(vp, t0, 2 * BLOCK, 1)
        mask = (s[None, :] <= t[:, None]) & (s[None, :] > t[:, None] - C_WINDOW) & (s[None, :] >= 0)
        logits = jnp.einsum('bqkgd,bskd->bkgqs', qb, kb).astype(jnp.float32) * scale
        logits = jnp.where(mask, logits, -jnp.inf)
        m = jnp.maximum(jnp.max(logits, axis=-1, keepdims=True), sink)
        e = jnp.exp(logits - m)
        den = jnp.sum(e, axis=-1, keepdims=True) + jnp.exp(sink - m)
        p = (e / den).astype(vb.dtype)
        o = jnp.einsum('bkgqs,bskd->bqkgd', p, vb)
        return o.reshape(Bsz, BLOCK, C_HEADS, HEAD_DIM)

    return over_query_blocks(block, L)


def hybrid_mixer(u, pos, w_in, idx_k_norm, sinks, w_a, w_b, w_c, w_o):
    Bsz, L, _ = u.shape
    (aq, ak, av, iq, ik, iw, bq, bk, bv, cq, ck, cv, gates) = split_columns(u @ w_in)
    qa = rope(aq.reshape(Bsz, L, A_HEADS, HEAD_DIM), pos)
    ka = rope(ak[:, :, None, :], pos)[:, :, 0]
    iqr = rope(iq.reshape(Bsz, L, IDX_HEADS, IDX_DIM), pos)
    ikr = rope(rmsnorm(ik, idx_k_norm)[:, :, None, :], pos)[:, :, 0]
    o_a = dsa_attention(qa, ka, av, iqr, ikr, iw * IDX_HEADS ** -0.5)
    o_b = dilated_attention(rope(bq.reshape(Bsz, L, B_HEADS, HEAD_DIM), pos),
                            rope(bk.reshape(Bsz, L, B_HEADS, HEAD_DIM), pos),
                            bv.reshape(Bsz, L, B_HEADS, HEAD_DIM))
    o_c = sink_window_attention(rope(cq.reshape(Bsz, L, C_HEADS, HEAD_DIM), pos),
                                rope(ck.reshape(Bsz, L, C_KV_HEADS, HEAD_DIM), pos),
                                cv.reshape(Bsz, L, C_KV_HEADS, HEAD_DIM), sinks)
    g = jax.nn.sigmoid(gates.reshape(Bsz, L, N_BRANCHES, D_MODEL))
    merged = (g[:, :, 0] * (o_a.reshape(Bsz, L, -1) @ w_a)
              + g[:, :, 1] * (o_b.reshape(Bsz, L, -1) @ w_b)
              + g[:, :, 2] * (o_c.reshape(Bsz, L, -1) @ w_c))
    return merged @ w_o


def setup_inputs(seed: int = 0) -> dict:
    key = jax.random.key(seed)
    ks = jax.random.split(key, 15)

    def normal(k, shape, scale):
        return jax.random.normal(k, shape, jnp.float32) * scale

    def gain(k, shape):
        return 1.0 + 0.05 * jax.random.normal(k, shape, jnp.float32)

    x = normal(ks[0], (BATCH, SEQ, D_MODEL), 1.0)
    positions = jnp.tile(jnp.arange(SEQ, dtype=jnp.int32)[None, :], (BATCH, 1))
    return {
        "x": x,
        "positions": positions,
        "attn_norm": gain(ks[1], (DEPTH, D_MODEL)),
        "w_in": normal(ks[2], (DEPTH, D_MODEL, IN_WIDTH), D_MODEL ** -0.5),
        "idx_k_norm": gain(ks[3], (DEPTH, IDX_DIM)),
        "sinks": normal(ks[4], (DEPTH, C_HEADS), 0.5),
        "w_a": normal(ks[5], (DEPTH, A_HEADS * HEAD_DIM, D_MODEL), (A_HEADS * HEAD_DIM) ** -0.5),
        "w_b": normal(ks[6], (DEPTH, B_HEADS_PER_GROUP * HEAD_DIM, D_MODEL), (B_HEADS_PER_GROUP * HEAD_DIM) ** -0.5),
        "w_c": normal(ks[7], (DEPTH, C_HEADS * HEAD_DIM, D_MODEL), (C_HEADS * HEAD_DIM) ** -0.5),
        "w_o": normal(ks[8], (DEPTH, D_MODEL, D_MODEL), D_MODEL ** -0.5),
        "mlp_norm": gain(ks[9], (DEPTH, D_MODEL)),
        "w_up": normal(ks[10], (DEPTH, D_MODEL, D_FF), D_MODEL ** -0.5),
        "w_down": normal(ks[11], (DEPTH, D_FF, D_MODEL), D_FF ** -0.5),
        "final_norm": gain(ks[12], (D_MODEL,)),
    }


def reference(x, positions, attn_norm, w_in, idx_k_norm, sinks, w_a, w_b, w_c, w_o,
              mlp_norm, w_up, w_down, final_norm):
    h = x
    for l in range(DEPTH):
        u = rmsnorm(h, attn_norm[l])
        h = h + hybrid_mixer(u, positions, w_in[l], idx_k_norm[l], sinks[l],
                             w_a[l], w_b[l], w_c[l], w_o[l])
        u = rmsnorm(h, mlp_norm[l])
        h = h + jnp.square(jax.nn.relu(u @ w_up[l])) @ w_down[l]
    return rmsnorm(h, final_norm)
```

```python
import functools

import jax
import jax.numpy as jnp
from jax import lax
from jax.experimental import pallas as pl
from jax.experimental.pallas import tpu as pltpu

D_MODEL = 1024
HEAD_DIM = 64
HALF = HEAD_DIM // 2
ROPE_THETA = 10000.0
NORM_EPS = 1e-6
A_HEADS = 6
IDX_HEADS = 8
IDX_DIM = 64
TOPK_MAX = 256
B_GROUPS = ((128, 1), (512, 4), (2048, 16))
B_HPG = 4
B_HEADS = B_HPG * len(B_GROUPS)
B_KEYS = 128
C_HEADS = 8
C_KV_HEADS = 2
C_GRP = C_HEADS // C_KV_HEADS
C_WINDOW = 128
N_BRANCHES = 3
D_FF = 4 * D_MODEL

BLOCK = 128
LANES = 128
INT_MIN = -2 ** 31
VMEM_LIMIT = 56 * 1024 * 1024

_W_AQ = A_HEADS * HEAD_DIM
_W_IQ = IDX_HEADS * IDX_DIM
_W_B = B_HEADS * HEAD_DIM
_W_CQ = C_HEADS * HEAD_DIM
_W_CKV = C_KV_HEADS * HEAD_DIM
_OFF = {}
_o = 0
for _name, _w in (("aq", _W_AQ), ("ak", HEAD_DIM), ("av", HEAD_DIM), ("iq", _W_IQ), ("ik", IDX_DIM),
                  ("iw", IDX_HEADS), ("bq", _W_B), ("bk", _W_B), ("bv", _W_B), ("cq", _W_CQ),
                  ("ck", _W_CKV), ("cv", _W_CKV), ("gates", N_BRANCHES * D_MODEL)):
    _OFF[_name] = (_o, _o + _w)
    _o += _w

_TOK_PIECES = (("kaik", 2 * HEAD_DIM), ("bq", _W_B), ("bk", _W_B), ("bv", _W_B),
               ("cq", _W_CQ), ("ck", _W_CKV), ("cv", _W_CKV))
_TOK_WIDTH = sum(w for _, w in _TOK_PIECES)
_TR_PIECES = (("qa", _W_AQ), ("iq", _W_IQ), ("va", HEAD_DIM), ("iw", IDX_HEADS))
_TR_WIDTH = sum(w for _, w in _TR_PIECES)

KC = 256


def _rmsnorm_f32(x, g):
    return x * lax.rsqrt(jnp.mean(x * x, axis=-1, keepdims=True) + NORM_EPS) * g


def _cparams(n_axes):
    return pltpu.CompilerParams(dimension_semantics=("arbitrary",) * n_axes, vmem_limit_bytes=VMEM_LIMIT)


def _inproj_kernel(x_ref, g_ref, wt_ref, wT_ref, cos_ref, sin_ref, cosT_ref, sinT_ref, gk_ref,
                   ka_ref, ik_ref, bq_ref, bk_ref, bv_ref, cq_ref, ck_ref, cv_ref,
                   qaT_ref, iqT_ref, vaT_ref, iwT_ref):
    tm = x_ref.shape[1]
    u = _rmsnorm_f32(x_ref[0], g_ref[...]).astype(jnp.bfloat16)
    cos = cos_ref[0]
    sin = sin_ref[0]
    lane = lax.broadcasted_iota(jnp.int32, (1, LANES), 1)
    first_half = (lane % HEAD_DIM) < HALF

    def rope_tile(z):
        rot = jnp.where(first_half, pltpu.roll(z, LANES - HALF, 1), pltpu.roll(z, HALF, 1))
        return z * cos + rot * sin

    def project(c0, width):
        return jnp.dot(u, wt_ref[:, c0:c0 + width], preferred_element_type=jnp.float32)

    c0 = 0
    z = project(c0, 2 * HEAD_DIM)
    is_ik = lane >= HEAD_DIM
    ms = jnp.sum(jnp.where(is_ik, z * z, 0.0), axis=-1, keepdims=True) * (1.0 / IDX_DIM)
    z = z * jnp.where(is_ik, lax.rsqrt(ms + NORM_EPS) * gk_ref[...], 1.0)
    z = rope_tile(z)
    ka_ref[0] = z[:, :HEAD_DIM].astype(ka_ref.dtype)
    ik_ref[0] = z[:, HEAD_DIM:].astype(ik_ref.dtype)
    c0 += 2 * HEAD_DIM

    for ref, width, roped in ((bq_ref, _W_B, True), (bk_ref, _W_B, True), (bv_ref, _W_B, False),
                              (cq_ref, _W_CQ, True), (ck_ref, _W_CKV, True), (cv_ref, _W_CKV, False)):
        z = project(c0, width)
        if roped:
            for t in range(width // LANES):
                ref[0, :, t * LANES:(t + 1) * LANES] = rope_tile(z[:, t * LANES:(t + 1) * LANES]).astype(ref.dtype)
        else:
            ref[0] = z.astype(ref.dtype)
        c0 += width

    def project_t(r0, rows):
        return lax.dot_general(wT_ref[r0:r0 + rows, :], u, (((1,), (1,)), ((), ())),
                               preferred_element_type=jnp.float32)

    cosT = cosT_ref[0]
    sinT = sinT_ref[0]

    def rope_rows(zt, ref, heads):
        for h in range(heads):
            x1 = zt[h * HEAD_DIM:h * HEAD_DIM + HALF]
            x2 = zt[h * HEAD_DIM + HALF:(h + 1) * HEAD_DIM]
            ref[0, h * HEAD_DIM:h * HEAD_DIM + HALF, :] = (x1 * cosT - x2 * sinT).astype(ref.dtype)
            ref[0, h * HEAD_DIM + HALF:(h + 1) * HEAD_DIM, :] = (x2 * cosT + x1 * sinT).astype(ref.dtype)

    r0 = 0
    rope_rows(project_t(r0, _W_AQ), qaT_ref, A_HEADS)
    r0 += _W_AQ
    rope_rows(project_t(r0, _W_IQ), iqT_ref, IDX_HEADS)
    r0 += _W_IQ
    zv = project_t(r0, HEAD_DIM)
    for c in range(tm // KC):
        vaT_ref[0, c] = zv[:, c * KC:(c + 1) * KC].astype(vaT_ref.dtype)
    r0 += HEAD_DIM
    iwT_ref[0] = project_t(r0, IDX_HEADS) * (IDX_HEADS ** -0.5)


def _inproj(h, g, wt, wT, tabs, gk, tm):
    B, L, D = h.shape
    nt = L // tm
    bf = jnp.bfloat16
    tok = lambda w: pl.BlockSpec((1, tm, w), lambda b, i: (b, i, 0))
    tr = lambda r: pl.BlockSpec((1, r, tm), lambda b, i: (b, 0, i))
    full2 = lambda a: pl.BlockSpec(a.shape, lambda b, i: (0, 0))
    out_shape = (
        jax.ShapeDtypeStruct((B, L, HEAD_DIM), bf), jax.ShapeDtypeStruct((B, L, IDX_DIM), bf),
        jax.ShapeDtypeStruct((B, L, _W_B), bf), jax.ShapeDtypeStruct((B, L, _W_B), bf),
        jax.ShapeDtypeStruct((B, L, _W_B), bf),
        jax.ShapeDtypeStruct((B, L, _W_CQ), bf), jax.ShapeDtypeStruct((B, L, _W_CKV), bf),
        jax.ShapeDtypeStruct((B, L, _W_CKV), bf),
        jax.ShapeDtypeStruct((B, _W_AQ, L), bf), jax.ShapeDtypeStruct((B, _W_IQ, L), bf),
        jax.ShapeDtypeStruct((B, L // KC, HEAD_DIM, KC), bf),
        jax.ShapeDtypeStruct((B, IDX_HEADS, L), jnp.float32),
    )
    out_specs = (
        tok(HEAD_DIM), tok(IDX_DIM), tok(_W_B), tok(_W_B), tok(_W_B), tok(_W_CQ), tok(_W_CKV), tok(_W_CKV),
        tr(_W_AQ), tr(_W_IQ),
        pl.BlockSpec((1, tm // KC, HEAD_DIM, KC), lambda b, i: (b, i, 0, 0)),
        tr(IDX_HEADS),
    )
    cos, sin, cosT, sinT = tabs
    return pl.pallas_call(
        _inproj_kernel,
        grid=(B, nt),
        in_specs=[tok(D), full2(g), full2(wt), full2(wT), tok(LANES), tok(LANES), tr(HALF), tr(HALF), full2(gk)],
        out_specs=out_specs,
        out_shape=out_shape,
        compiler_params=_cparams(2),
        name="inproj",
    )(h, g, wt, wT, cos, sin, cosT, sinT, gk)


def _dsa_kernel(ik_ref, ka_ref, vaT_ref, iqT_ref, qaT_ref, iwT_ref, o_ref,
                key_ref, m_ref, l_ref, acc_ref, outT_ref, *, topk):
    tq = o_ref.shape[1]
    qi = pl.program_id(1)
    nch = qi + 1
    col = qi * tq + lax.broadcasted_iota(jnp.int32, (KC, tq), 1)
    row0 = lax.broadcasted_iota(jnp.int32, (KC, tq), 0)

    def score_body(j, carry):
        s0 = pl.multiple_of(j * KC, KC)
        ikc = ik_ref[0, pl.ds(s0, KC), :]
        acc = jnp.zeros((KC, tq), jnp.float32)
        for h in range(IDX_HEADS):
            r = jnp.dot(ikc, iqT_ref[0, h * IDX_DIM:(h + 1) * IDX_DIM, :], preferred_element_type=jnp.float32)
            acc = acc + jnp.maximum(r, 0.0) * iwT_ref[0, h:h + 1, :]
        bits = lax.bitcast_convert_type(acc, jnp.int32)
        key = jnp.where(bits < 0, INT_MIN - bits, bits)
        key_ref[pl.ds(s0, KC), :] = jnp.where(row0 + s0 <= col, key, INT_MIN)
        return carry

    lax.fori_loop(0, nch, score_body, 0)

    def bit_body(i, t_u):
        cand_u = t_u | lax.shift_left(jnp.int32(1), 31 - i)
        cand_s = cand_u ^ INT_MIN

        def cnt_body(j, c):
            s0 = pl.multiple_of(j * KC, KC)
            ge = (key_ref[pl.ds(s0, KC), :] >= cand_s).astype(jnp.int32)
            return c + jnp.sum(ge, axis=0, keepdims=True)

        c = lax.fori_loop(0, nch, cnt_body, jnp.zeros((1, tq), jnp.int32))
        return jnp.where(c >= topk, cand_u, t_u)

    t_u = lax.fori_loop(0, 32, bit_body, jnp.zeros((1, tq), jnp.int32))
    thr = jnp.maximum(t_u ^ INT_MIN, INT_MIN + 1)

    m_ref[...] = jnp.full(m_ref.shape, -1e30, jnp.float32)
    l_ref[...] = jnp.zeros(l_ref.shape, jnp.float32)
    acc_ref[...] = jnp.zeros(acc_ref.shape, jnp.float32)

    def att_body(j, carry):
        s0 = pl.multiple_of(j * KC, KC)
        sel = key_ref[pl.ds(s0, KC), :] >= thr
        kc = ka_ref[0, pl.ds(s0, KC), :]
        vT = vaT_ref[0, j]
        for h in range(A_HEADS):
            lg = jnp.dot(kc, qaT_ref[0, h * HEAD_DIM:(h + 1) * HEAD_DIM, :], preferred_element_type=jnp.float32)
            lg = jnp.where(sel, lg, -jnp.inf)
            m_old = m_ref[h]
            m_new = jnp.maximum(m_old, jnp.max(lg, axis=0, keepdims=True))
            p = jnp.exp(lg - m_new)
            alpha = jnp.exp(m_old - m_new)
            l_ref[h] = alpha * l_ref[h] + jnp.sum(p, axis=0, keepdims=True)
            acc_ref[h] = alpha * acc_ref[h] + jnp.dot(vT, p.astype(vT.dtype), preferred_element_type=jnp.float32)
            m_ref[h] = m_new
        return carry

    lax.fori_loop(0, nch, att_body, 0)

    for h in range(A_HEADS):
        outT_ref[h * HEAD_DIM:(h + 1) * HEAD_DIM, :] = acc_ref[h] * (1.0 / l_ref[h])
    o_ref[0] = outT_ref[...].T.astype(o_ref.dtype)


def _dsa(ik, ka, vaT, iqT, qaT, iwT):
    B, L, _ = ik.shape
    tq = KC
    topk = min(TOPK_MAX, L // 4)
    seq = lambda w: pl.BlockSpec((1, L, w), lambda b, i: (b, 0, 0))
    tr = lambda r: pl.BlockSpec((1, r, tq), lambda b, i: (b, 0, i))
    return pl.pallas_call(
        functools.partial(_dsa_kernel, topk=topk),
        grid=(B, L // tq),
        in_specs=[seq(IDX_DIM), seq(HEAD_DIM),
                  pl.BlockSpec((1, L // KC, HEAD_DIM, KC), lambda b, i: (b, 0, 0, 0)),
                  tr(_W_IQ), tr(_W_AQ), tr(IDX_HEADS)],
        out_specs=pl.BlockSpec((1, tq, _W_AQ), lambda b, i: (b, i, 0)),
        out_shape=jax.ShapeDtypeStruct((B, L, _W_AQ), jnp.bfloat16),
        scratch_shapes=[pltpu.VMEM((L, tq), jnp.int32),
                        pltpu.VMEM((A_HEADS, 1, tq), jnp.float32),
                        pltpu.VMEM((A_HEADS, 1, tq), jnp.float32),
                        pltpu.VMEM((A_HEADS, HEAD_DIM, tq), jnp.float32),
                        pltpu.VMEM((_W_AQ, tq), jnp.float32)],
        compiler_params=_cparams(2),
        name="dsa",
    )(ik, ka, vaT, iqT, qaT, iwT)


def _dilated_kernel(q_ref, kc_ref, kp_ref, vc_ref, vp_ref, o_ref, lse_ref):
    tqb = q_ref.shape[1]
    w = q_ref.shape[2]
    ib = pl.program_id(2)
    q = q_ref[0]
    kcat = jnp.concatenate([kp_ref[0], kc_ref[0]], axis=0)
    vcat = jnp.concatenate([vp_ref[0], vc_ref[0]], axis=0)
    head_of_lane = lax.broadcasted_iota(jnp.int32, (1, w), 1) // HEAD_DIM
    rr = lax.broadcasted_iota(jnp.int32, (BLOCK, 2 * BLOCK), 0)
    cc = lax.broadcasted_iota(jnp.int32, (BLOCK, 2 * BLOCK), 1)
    band = (cc >= rr) & (cc <= rr + B_KEYS)
    for sb in range(tqb // BLOCK):
        qs = q[sb * BLOCK:(sb + 1) * BLOCK]
        ks = kcat[sb * BLOCK:sb * BLOCK + 2 * BLOCK]
        vs = vcat[sb * BLOCK:sb * BLOCK + 2 * BLOCK]
        mask = band
        if sb == 0:
            mask = band & ((cc >= BLOCK) | (ib > 0))
        o_acc = jnp.zeros((BLOCK, w), jnp.float32)
        lse_acc = jnp.zeros((BLOCK, w), jnp.float32)
        for hh in range(B_HPG):
            hm = head_of_lane == hh
            qm = jnp.where(hm, qs, jnp.zeros_like(qs))
            lg = lax.dot_general(qm, ks, (((1,), (1,)), ((), ())), preferred_element_type=jnp.float32)
            lg = jnp.where(mask, lg, -jnp.inf)
            m = jnp.max(lg, axis=-1, keepdims=True)
            e = jnp.exp(lg - m)
            den = jnp.sum(e, axis=-1, keepdims=True)
            pv = jnp.dot(e.astype(vs.dtype), vs, preferred_element_type=jnp.float32)
            o_acc = jnp.where(hm, pv * (1.0 / den), o_acc)
            lse_acc = jnp.where(hm, m + jnp.log(den), lse_acc)
        o_ref[0, sb * BLOCK:(sb + 1) * BLOCK, :] = o_acc
        lse_ref[0, sb * BLOCK:(sb + 1) * BLOCK, :] = lse_acc


def _dilated_group(bq, bk, bv, g, dil):
    B, L, _ = bq.shape
    ld = L // dil
    w = B_HPG * HEAD_DIM
    ncol = _W_B // w
    tqb = min(512, ld)
    per = tqb // BLOCK
    view = lambda a: a.reshape(B, ld, dil * _W_B)
    cur = pl.BlockSpec((1, tqb, w), lambda b, r, i: (b, i, r * ncol + g))
    prev = pl.BlockSpec((1, BLOCK, w), lambda b, r, i: (b, jnp.maximum(i * per - 1, 0), r * ncol + g))
    out = pl.BlockSpec((1, tqb, w), lambda b, r, i: (b, i, r))
    o, lse = pl.pallas_call(
        _dilated_kernel,
        grid=(B, dil, ld // tqb),
        in_specs=[cur, cur, prev, cur, prev],
        out_specs=(out, out),
        out_shape=(jax.ShapeDtypeStruct((B, ld, dil * w), jnp.float32),) * 2,
        compiler_params=_cparams(3),
        name=f"dilated{dil}",
    )(view(bq), view(bk), view(bk), view(bv), view(bv))
    return o.reshape(B, L, w), lse.reshape(B, L, w)


def _sinkwin_kernel(sink_ref, q_ref, kc_ref, kp_ref, vc_ref, vp_ref, o_ref):
    tqb = q_ref.shape[1]
    ib = pl.program_id(1)
    q = q_ref[0]
    kcat = jnp.concatenate([kp_ref[0], kc_ref[0]], axis=0)
    vcat = jnp.concatenate([vp_ref[0], vc_ref[0]], axis=0)
    kv_of_lane = lax.broadcasted_iota(jnp.int32, (1, LANES), 1) // HEAD_DIM
    rr = lax.broadcasted_iota(jnp.int32, (BLOCK, 2 * BLOCK), 0)
    cc = lax.broadcasted_iota(jnp.int32, (BLOCK, 2 * BLOCK), 1)
    band = (cc > rr) & (cc <= rr + C_WINDOW)
    for sb in range(tqb // BLOCK):
        ks = kcat[sb * BLOCK:sb * BLOCK + 2 * BLOCK]
        vs = vcat[sb * BLOCK:sb * BLOCK + 2 * BLOCK]
        mask = band
        if sb == 0:
            mask = band & ((cc >= BLOCK) | (ib > 0))
        for g in range(C_GRP):
            qs = q[sb * BLOCK:(sb + 1) * BLOCK, g * LANES:(g + 1) * LANES]
            o_acc = jnp.zeros((BLOCK, LANES), jnp.float32)
            for kk in range(C_KV_HEADS):
                hm = kv_of_lane == kk
                sink = sink_ref[kk * C_GRP + g]
                qm = jnp.where(hm, qs, jnp.zeros_like(qs))
                lg = lax.dot_general(qm, ks, (((1,), (1,)), ((), ())), preferred_element_type=jnp.float32)
                lg = jnp.where(mask, lg, -jnp.inf)
                m = jnp.maximum(jnp.max(lg, axis=-1, keepdims=True), sink)
                e = jnp.exp(lg - m)
                den = jnp.sum(e, axis=-1, keepdims=True) + jnp.exp(sink - m)
                pv = jnp.dot(e.astype(vs.dtype), vs, preferred_element_type=jnp.float32)
                o_acc = jnp.where(hm, pv * (1.0 / den), o_acc)
            o_ref[0, sb * BLOCK:(sb + 1) * BLOCK, g * LANES:(g + 1) * LANES] = o_acc.astype(o_ref.dtype)


def _sinkwin(cq, ck, cv, sinks):
    B, L, _ = cq.shape
    tqb = min(512, L)
    per = tqb // BLOCK
    cur = lambda w: pl.BlockSpec((1, tqb, w), lambda b, i: (b, i, 0))
    prev = lambda w: pl.BlockSpec((1, BLOCK, w), lambda b, i: (b, jnp.maximum(i * per - 1, 0), 0))
    return pl.pallas_call(
        _sinkwin_kernel,
        grid=(B, L // tqb),
        in_specs=[pl.BlockSpec(memory_space=pltpu.SMEM),
                  cur(_W_CQ), cur(_W_CKV), prev(_W_CKV), cur(_W_CKV), prev(_W_CKV)],
        out_specs=cur(_W_CQ),
        out_shape=jax.ShapeDtypeStruct((B, L, _W_CQ), jnp.bfloat16),
        compiler_params=_cparams(2),
        name="sinkwin",
    )(sinks, cq, ck, ck, cv, cv)


def _mixer_out_kernel(h_ref, g_ref, wg_ref, oa_ref, ob0_ref, ob1_ref, ob2_ref, l0_ref, l1_ref, l2_ref,
                      oc_ref, wa_ref, wb_ref, wc_ref, wo_ref, out_ref):
    h = h_ref[...]
    u = _rmsnorm_f32(h, g_ref[...]).astype(jnp.bfloat16)
    l0, l1, l2 = l0_ref[...], l1_ref[...], l2_ref[...]
    mx = jnp.maximum(jnp.maximum(l0, l1), l2)
    e0, e1, e2 = jnp.exp(l0 - mx), jnp.exp(l1 - mx), jnp.exp(l2 - mx)
    ob = (e0 * ob0_ref[...] + e1 * ob1_ref[...] + e2 * ob2_ref[...]) * (1.0 / (e0 + e1 + e2))

    def gate(i):
        z = jnp.dot(u, wg_ref[:, i * D_MODEL:(i + 1) * D_MODEL], preferred_element_type=jnp.float32)
        return jax.nn.sigmoid(z)

    merged = gate(0) * jnp.dot(oa_ref[...], wa_ref[...], preferred_element_type=jnp.float32)
    merged = merged + gate(1) * jnp.dot(ob.astype(jnp.bfloat16), wb_ref[...], preferred_element_type=jnp.float32)
    merged = merged + gate(2) * jnp.dot(oc_ref[...], wc_ref[...], preferred_element_type=jnp.float32)
    out_ref[...] = h + jnp.dot(merged.astype(jnp.bfloat16), wo_ref[...], preferred_element_type=jnp.float32)


def _mixer_out(h2d, g, wg, oa, obs, lses, oc, wa, wb, wc, wo, tm):
    N, D = h2d.shape
    row = lambda w: pl.BlockSpec((tm, w), lambda i: (i, 0))
    full = lambda a: pl.BlockSpec(a.shape, lambda i: (0, 0))
    wbh = B_HPG * HEAD_DIM
    return pl.pallas_call(
        _mixer_out_kernel,
        grid=(N // tm,),
        in_specs=[row(D), full(g), full(wg), row(_W_AQ), row(wbh), row(wbh), row(wbh), row(wbh), row(wbh), row(wbh),
                  row(_W_CQ), full(wa), full(wb), full(wc), full(wo)],
        out_specs=row(D),
        out_shape=jax.ShapeDtypeStruct((N, D), jnp.float32),
        compiler_params=_cparams(1),
        name="mixer_out",
    )(h2d, g, wg, oa, *obs, *lses, oc, wa, wb, wc, wo)


def _ffn_kernel(h_ref, g_ref, wup_ref, wdn_ref, gf_ref, out_ref, *, final, chunk):
    h = h_ref[...]
    u = _rmsnorm_f32(h, g_ref[...]).astype(jnp.bfloat16)
    acc = h
    for c in range(D_FF // chunk):
        up = jnp.dot(u, wup_ref[:, c * chunk:(c + 1) * chunk], preferred_element_type=jnp.float32)
        act = jnp.square(jnp.maximum(up, 0.0)).astype(jnp.bfloat16)
        acc = acc + jnp.dot(act, wdn_ref[c * chunk:(c + 1) * chunk, :], preferred_element_type=jnp.float32)
    if final:
        acc = _rmsnorm_f32(acc, gf_ref[...])
    out_ref[...] = acc


def _ffn(h2d, g, wup, wdn, gf, final, tm):
    N, D = h2d.shape
    row = pl.BlockSpec((tm, D), lambda i: (i, 0))
    full = lambda a: pl.BlockSpec(a.shape, lambda i: (0, 0))
    return pl.pallas_call(
        functools.partial(_ffn_kernel, final=final, chunk=1024),
        grid=(N // tm,),
        in_specs=[row, full(g), full(wup), full(wdn), full(gf)],
        out_specs=row,
        out_shape=jax.ShapeDtypeStruct((N, D), jnp.float32),
        compiler_params=_cparams(1),
        name="ffn",
    )(h2d, g, wup, wdn, gf)


def _rope_tables(positions):
    inv = ROPE_THETA ** (-jnp.arange(HALF, dtype=jnp.float32) / HALF)
    ang = positions.astype(jnp.float32)[..., None] * inv
    c, s = jnp.cos(ang), jnp.sin(ang)
    reps = LANES // HEAD_DIM
    cos = jnp.tile(jnp.concatenate([c, c], axis=-1), (1, 1, reps))
    sin = jnp.tile(jnp.concatenate([-s, s], axis=-1), (1, 1, reps))
    return cos, sin, jnp.swapaxes(c, 1, 2), jnp.swapaxes(s, 1, 2)


def _layer_weights(w_in, idx_k_norm, w_c):
    col = lambda n: w_in[:, _OFF[n][0]:_OFF[n][1]]
    scale = HEAD_DIM ** -0.5
    idx_scale = IDX_DIM ** -0.5
    cq = col("cq").reshape(D_MODEL, C_KV_HEADS, C_GRP, HEAD_DIM).transpose(0, 2, 1, 3).reshape(D_MODEL, _W_CQ)
    wt = jnp.concatenate([col("ak"), col("ik"), col("bq") * scale, col("bk"), col("bv"),
                          cq * scale, col("ck"), col("cv")], axis=1).astype(jnp.bfloat16)
    wT = jnp.concatenate([col("aq") * scale, col("iq") * idx_scale, col("av"), col("iw")],
                         axis=1).T.astype(jnp.bfloat16)
    wg = col("gates").astype(jnp.bfloat16)
    gk = jnp.concatenate([jnp.ones((IDX_DIM,), jnp.float32), idx_k_norm])[None, :]
    wc = w_c.reshape(C_KV_HEADS, C_GRP, HEAD_DIM, D_MODEL).transpose(1, 0, 2, 3).reshape(_W_CQ, D_MODEL)
    return wt, wT, wg, gk, wc.astype(jnp.bfloat16)


def kernel(x, positions, attn_norm, w_in, idx_k_norm, sinks, w_a, w_b, w_c, w_o, mlp_norm, w_up, w_down, final_norm):
    B, L, D = x.shape
    depth = w_in.shape[0]
    assert D == D_MODEL and L % (16 * BLOCK) == 0
    tabs = _rope_tables(positions)
    bf = jnp.bfloat16
    tm = 512
    h = x
    for l in range(depth):
        wt, wT, wg, gk, wc = _layer_weights(w_in[l], idx_k_norm[l], w_c[l])
        (ka, ik, bq, bk, bv, cq, ck, cv, qaT, iqT, vaT, iwT) = _inproj(
            h, attn_norm[l][None, :], wt, wT, tabs, gk, tm)
        oa = _dsa(ik, ka, vaT, iqT, qaT, iwT)
        obs, lses = [], []
        for g, (_, dil) in enumerate(B_GROUPS):
            o, lse = _dilated_group(bq, bk, bv, g, dil)
            obs.append(o.reshape(B * L, -1))
            lses.append(lse.reshape(B * L, -1))
        oc = _sinkwin(cq, ck, cv, sinks[l])
        h2 = _mixer_out(h.reshape(B * L, D), attn_norm[l][None, :], wg, oa.reshape(B * L, -1), obs, lses,
                        oc.reshape(B * L, -1), w_a[l].astype(bf), w_b[l].astype(bf), wc, w_o[l].astype(bf), tm)
        h3 = _ffn(h2, mlp_norm[l][None, :], w_up[l].astype(bf), w_down[l].astype(bf), final_norm[None, :],
                  l == depth - 1, tm)
        h = h3.reshape(B, L, D)
    return h
```

```python
import functools

import jax
import jax.numpy as jnp
from jax import lax
from jax.experimental import pallas as pl
from jax.experimental.pallas import tpu as pltpu

D_MODEL = 1024
HEAD_DIM = 64
HALF = HEAD_DIM // 2
ROPE_THETA = 10000.0
NORM_EPS = 1e-6
A_HEADS = 6
IDX_HEADS = 8
IDX_DIM = 64
TOPK_MAX = 256
B_GROUPS = ((128, 1), (512, 4), (2048, 16))
B_HPG = 4
B_HEADS = B_HPG * len(B_GROUPS)
B_GW = B_HPG * HEAD_DIM
B_KEYS = 128
C_HEADS = 8
C_KV_HEADS = 2
C_GRP = C_HEADS // C_KV_HEADS
C_WINDOW = 128
N_BRANCHES = 3
D_FF = 4 * D_MODEL

BLOCK = 128
LANES = 128
INT_MIN = -2 ** 31
VMEM_LIMIT = 56 * 1024 * 1024

_W_AQ = A_HEADS * HEAD_DIM
_W_IQ = IDX_HEADS * IDX_DIM
_W_B = B_HEADS * HEAD_DIM
_W_CQ = C_HEADS * HEAD_DIM
_W_CKV = C_KV_HEADS * HEAD_DIM
_OFF = {}
_o = 0
for _name, _w in (("aq", _W_AQ), ("ak", HEAD_DIM), ("av", HEAD_DIM), ("iq", _W_IQ), ("ik", IDX_DIM),
                  ("iw", IDX_HEADS), ("bq", _W_B), ("bk", _W_B), ("bv", _W_B), ("cq", _W_CQ),
                  ("ck", _W_CKV), ("cv", _W_CKV), ("gates", N_BRANCHES * D_MODEL)):
    _OFF[_name] = (_o, _o + _w)
    _o += _w

KC = 256
TM = 512


def _rmsnorm_f32(x, g):
    return x * lax.rsqrt(jnp.mean(x * x, axis=-1, keepdims=True) + NORM_EPS) * g


def _cparams(n_axes):
    return pltpu.CompilerParams(dimension_semantics=("arbitrary",) * n_axes, vmem_limit_bytes=VMEM_LIMIT)


def _inproj_kernel(x_ref, g_ref, wt_ref, wT_ref, cos_ref, sin_ref, cosT_ref, sinT_ref, gk_ref,
                   ka_ref, ik_ref,
                   bq0_ref, bq1_ref, bq2_ref, bk0_ref, bk1_ref, bk2_ref, bv0_ref, bv1_ref, bv2_ref,
                   cq_ref, ck_ref, cv_ref, qa_ref, iq_ref, va_ref, iw_ref, stage_ref):
    tm = x_ref.shape[1]
    u = _rmsnorm_f32(x_ref[0], g_ref[...]).astype(jnp.bfloat16)
    cos = cos_ref[0]
    sin = sin_ref[0]
    lane = lax.broadcasted_iota(jnp.int32, (1, LANES), 1)
    first_half = (lane % HEAD_DIM) < HALF

    def rope_tile(z):
        rot = jnp.where(first_half, pltpu.roll(z, LANES - HALF, 1), pltpu.roll(z, HALF, 1))
        return z * cos + rot * sin

    def project(c0, width):
        return jnp.dot(u, wt_ref[:, c0:c0 + width], preferred_element_type=jnp.float32)

    c0 = 0
    z = project(c0, 2 * HEAD_DIM)
    is_ik = lane >= HEAD_DIM
    ms = jnp.sum(jnp.where(is_ik, z * z, 0.0), axis=-1, keepdims=True) * (1.0 / IDX_DIM)
    z = z * jnp.where(is_ik, lax.rsqrt(ms + NORM_EPS) * gk_ref[...], 1.0)
    z = rope_tile(z)
    ka_ref[0] = z[:, :HEAD_DIM].astype(ka_ref.dtype)
    ik_ref[0] = z[:, HEAD_DIM:].astype(ik_ref.dtype)
    c0 += 2 * HEAD_DIM

    for refs, roped in (((bq0_ref, bq1_ref, bq2_ref), True), ((bk0_ref, bk1_ref, bk2_ref), True),
                        ((bv0_ref, bv1_ref, bv2_ref), False)):
        for g, (ref, (_, dil)) in enumerate(zip(refs, B_GROUPS)):
            z = project(c0 + g * B_GW, B_GW)
            if roped:
                z = jnp.concatenate([rope_tile(z[:, t * LANES:(t + 1) * LANES]) for t in range(B_GW // LANES)], axis=1)
            if dil == 1:
                ref[0, 0] = z.astype(ref.dtype)
            else:
                for t in range(B_GW // LANES):
                    stage_ref[t] = z[:, t * LANES:(t + 1) * LANES]
                for r in range(dil):
                    for t in range(B_GW // LANES):
                        ref[0, r, :, t * LANES:(t + 1) * LANES] = (
                            stage_ref[t, pl.ds(r, tm // dil, stride=dil), :].astype(ref.dtype))
        c0 += _W_B

    for ref, width, roped in ((cq_ref, _W_CQ, True), (ck_ref, _W_CKV, True), (cv_ref, _W_CKV, False)):
        z = project(c0, width)
        if roped:
            for t in range(width // LANES):
                ref[0, :, t * LANES:(t + 1) * LANES] = rope_tile(z[:, t * LANES:(t + 1) * LANES]).astype(ref.dtype)
        else:
            ref[0] = z.astype(ref.dtype)
        c0 += width

    def project_t(r0, rows):
        return lax.dot_general(wT_ref[r0:r0 + rows, :], u, (((1,), (1,)), ((), ())),
                               preferred_element_type=jnp.float32)

    cosT = cosT_ref[0]
    sinT = sinT_ref[0]

    def rope_rows(zt, ref, heads):
        for h in range(heads):
            x1 = zt[h * HEAD_DIM:h * HEAD_DIM + HALF]
            x2 = zt[h * HEAD_DIM + HALF:(h + 1) * HEAD_DIM]
            o1 = (x1 * cosT - x2 * sinT).astype(ref.dtype)
            o2 = (x2 * cosT + x1 * sinT).astype(ref.dtype)
            for c in range(tm // KC):
                ref[0, c, 0:HALF, h * KC:(h + 1) * KC] = o1[:, c * KC:(c + 1) * KC]
                ref[0, c, HALF:HEAD_DIM, h * KC:(h + 1) * KC] = o2[:, c * KC:(c + 1) * KC]

    r0 = 0
    rope_rows(project_t(r0, _W_AQ), qa_ref, A_HEADS)
    r0 += _W_AQ
    rope_rows(project_t(r0, _W_IQ), iq_ref, IDX_HEADS)
    r0 += _W_IQ
    zv = project_t(r0, HEAD_DIM)
    for c in range(tm // KC):
        va_ref[0, c] = zv[:, c * KC:(c + 1) * KC].astype(va_ref.dtype)
    r0 += HEAD_DIM
    zw = project_t(r0, IDX_HEADS) * (IDX_HEADS ** -0.5)
    for c in range(tm // KC):
        for h in range(IDX_HEADS):
            iw_ref[0, c, :, h * KC:(h + 1) * KC] = zw[h:h + 1, c * KC:(c + 1) * KC]


def _inproj(h, g, wt, wT, tabs, gk):
    B, L, D = h.shape
    tm = TM
    nq = L // KC
    bf = jnp.bfloat16
    tok = lambda w: pl.BlockSpec((1, tm, w), lambda b, i: (b, i, 0))
    tr = lambda r: pl.BlockSpec((1, r, tm), lambda b, i: (b, 0, i))
    full2 = lambda a: pl.BlockSpec(a.shape, lambda b, i: (0, 0))
    qtile = lambda rows, w: pl.BlockSpec((1, tm // KC, rows, w), lambda b, i: (b, i, 0, 0))
    grp_shape = lambda dil: jax.ShapeDtypeStruct((B, dil, L // dil, B_GW), bf)
    grp_spec = lambda dil: pl.BlockSpec((1, dil, tm // dil, B_GW), lambda b, i: (b, 0, i, 0))
    dils = [d for _, d in B_GROUPS]
    out_shape = (
        [jax.ShapeDtypeStruct((B, L, HEAD_DIM), bf), jax.ShapeDtypeStruct((B, L, IDX_DIM), bf)]
        + [grp_shape(d) for d in dils] * 3
        + [jax.ShapeDtypeStruct((B, L, _W_CQ), bf), jax.ShapeDtypeStruct((B, L, _W_CKV), bf),
           jax.ShapeDtypeStruct((B, L, _W_CKV), bf),
           jax.ShapeDtypeStruct((B, nq, HEAD_DIM, A_HEADS * KC), bf),
           jax.ShapeDtypeStruct((B, nq, IDX_DIM, IDX_HEADS * KC), bf),
           jax.ShapeDtypeStruct((B, nq, HEAD_DIM, KC), bf),
           jax.ShapeDtypeStruct((B, nq, 1, IDX_HEADS * KC), jnp.float32)])
    out_specs = (
        [tok(HEAD_DIM), tok(IDX_DIM)] + [grp_spec(d) for d in dils] * 3
        + [tok(_W_CQ), tok(_W_CKV), tok(_W_CKV),
           qtile(HEAD_DIM, A_HEADS * KC), qtile(IDX_DIM, IDX_HEADS * KC), qtile(HEAD_DIM, KC),
           qtile(1, IDX_HEADS * KC)])
    cos, sin, cosT, sinT = tabs
    return pl.pallas_call(
        _inproj_kernel,
        grid=(B, L // tm),
        in_specs=[tok(D), full2(g), full2(wt), full2(wT), tok(LANES), tok(LANES), tr(HALF), tr(HALF), full2(gk)],
        out_specs=out_specs,
        out_shape=out_shape,
        scratch_shapes=[pltpu.VMEM((B_GW // LANES, tm, LANES), jnp.float32)],
        compiler_params=_cparams(2),
        name="inproj",
    )(h, g, wt, wT, cos, sin, cosT, sinT, gk)


def _dsa_kernel(ik_ref, ka_ref, va_ref, iq_ref, qa_ref, iw_ref, o_ref,
                key_ref, m_ref, l_ref, acc_ref, outT_ref, *, topk):
    tq = o_ref.shape[1]
    qi = pl.program_id(1)
    nch = qi + 1
    npair = (nch + 1) // 2
    col = qi * tq + lax.broadcasted_iota(jnp.int32, (KC, tq), 1)
    row0 = lax.broadcasted_iota(jnp.int32, (KC, tq), 0)

    def score_body(j, carry):
        s0 = pl.multiple_of(j * KC, KC)
        r = jnp.dot(ik_ref[0, pl.ds(s0, KC), :], iq_ref[0, 0], preferred_element_type=jnp.float32)
        wr = jnp.maximum(r, 0.0) * iw_ref[0, 0]
        acc = wr[:, 0:tq]
        for h in range(1, IDX_HEADS):
            acc = acc + wr[:, h * tq:(h + 1) * tq]
        bits = lax.bitcast_convert_type(acc, jnp.int32)
        key = jnp.where(bits < 0, INT_MIN - bits, bits)
        key_ref[pl.ds(s0, KC), :] = jnp.where(row0 + s0 <= col, key, INT_MIN)
        return carry

    lax.fori_loop(0, nch, score_body, 0)

    @pl.when(nch % 2 == 1)
    def _():
        key_ref[pl.ds(pl.multiple_of(nch * KC, KC), KC), :] = jnp.full((KC, tq), INT_MIN, jnp.int32)

    def count_ge(cand):
        def cnt_body(j, c):
            s0 = pl.multiple_of(j * (2 * KC), 2 * KC)
            ge = (key_ref[pl.ds(s0, 2 * KC), :] >= cand).astype(jnp.int32)
            return c + jnp.sum(ge.reshape(2 * KC // 8, 8, tq), axis=0)
        c8 = lax.fori_loop(0, npair, cnt_body, jnp.zeros((8, tq), jnp.int32))
        return jnp.sum(c8, axis=0, keepdims=True)

    n_valid = qi * tq + lax.broadcasted_iota(jnp.int32, (1, tq), 1) + 1

    def search_cond(st):
        i, _, cnt_t = st
        pending = jnp.logical_and(n_valid > topk, cnt_t != topk)
        return jnp.logical_and(i < 32, jnp.max(pending.astype(jnp.int32)) > 0)

    def search_body(st):
        i, t_u, cnt_t = st
        cand_u = t_u | lax.shift_left(jnp.int32(1), 31 - i)
        c = count_ge(cand_u ^ INT_MIN)
        take = c >= topk
        return i + 1, jnp.where(take, cand_u, t_u), jnp.where(take, c, cnt_t)

    _, t_u, cnt_t = lax.while_loop(search_cond, search_body,
                                   (jnp.int32(0), jnp.zeros((1, tq), jnp.int32), n_valid))
    thr = jnp.maximum(t_u ^ INT_MIN, INT_MIN + 1)

    @pl.when(jnp.max((jnp.logical_and(n_valid > topk, cnt_t > topk)).astype(jnp.int32)) > 0)
    def _():
        need = topk - count_ge(thr + 1)
        tri = (lax.broadcasted_iota(jnp.int32, (KC, KC), 1)
               < lax.broadcasted_iota(jnp.int32, (KC, KC), 0)).astype(jnp.bfloat16)

        def tie_body(j, run):
            s0 = pl.multiple_of(j * KC, KC)
            k = key_ref[pl.ds(s0, KC), :]
            eq = k == thr
            eqf = eq.astype(jnp.bfloat16)
            before = run + jnp.dot(tri, eqf, preferred_element_type=jnp.float32)
            drop = jnp.logical_and(eq, before >= need.astype(jnp.float32))
            key_ref[pl.ds(s0, KC), :] = jnp.where(drop, INT_MIN, k)
            return run + jnp.sum(eq.astype(jnp.float32), axis=0, keepdims=True)

        lax.fori_loop(0, nch, tie_body, jnp.zeros((1, tq), jnp.float32))

    m_ref[...] = jnp.full(m_ref.shape, -1e30, jnp.float32)
    l_ref[...] = jnp.zeros(l_ref.shape, jnp.float32)
    acc_ref[...] = jnp.zeros(acc_ref.shape, jnp.float32)

    def att_body(j, carry):
        s0 = pl.multiple_of(j * KC, KC)
        sel = key_ref[pl.ds(s0, KC), :] >= thr
        lg = jnp.dot(ka_ref[0, pl.ds(s0, KC), :], qa_ref[0, 0], preferred_element_type=jnp.float32)
        m_old = m_ref[...]
        ps, m_news = [], []
        for h in range(A_HEADS):
            lgh = jnp.where(sel, lg[:, h * tq:(h + 1) * tq], -jnp.inf)
            m_new = jnp.maximum(m_old[:, h * tq:(h + 1) * tq], jnp.max(lgh, axis=0, keepdims=True))
            ps.append(jnp.exp(lgh - m_new))
            m_news.append(m_new)
        p = jnp.concatenate(ps, axis=1)
        m_new = jnp.concatenate(m_news, axis=1)
        alpha = jnp.exp(m_old - m_new)
        l_ref[...] = alpha * l_ref[...] + jnp.sum(p, axis=0, keepdims=True)
        pv = jnp.dot(va_ref[0, j], p.astype(jnp.bfloat16), preferred_element_type=jnp.float32)
        acc_ref[...] = alpha * acc_ref[...] + pv
        m_ref[...] = m_new
        return carry

    lax.fori_loop(0, nch, att_body, 0)

    out = acc_ref[...] * (1.0 / l_ref[...])
    for h in range(A_HEADS):
        outT_ref[h * HEAD_DIM:(h + 1) * HEAD_DIM, :] = out[:, h * tq:(h + 1) * tq]
    o_ref[0] = outT_ref[...].T.astype(o_ref.dtype)


def _dsa(ik, ka, va, iq, qa, iw):
    B, L, _ = ik.shape
    tq = KC
    topk = min(TOPK_MAX, L // 4)
    seq = lambda w: pl.BlockSpec((1, L, w), lambda b, i: (b, 0, 0))
    qtile = lambda rows, w: pl.BlockSpec((1, 1, rows, w), lambda b, i: (b, i, 0, 0))
    return pl.pallas_call(
        functools.partial(_dsa_kernel, topk=topk),
        grid=(B, L // tq),
        in_specs=[seq(IDX_DIM), seq(HEAD_DIM),
                  pl.BlockSpec((1, L // KC, HEAD_DIM, KC), lambda b, i: (b, 0, 0, 0)),
                  qtile(IDX_DIM, IDX_HEADS * tq), qtile(HEAD_DIM, A_HEADS * tq), qtile(1, IDX_HEADS * tq)],
        out_specs=pl.BlockSpec((1, tq, _W_AQ), lambda b, i: (b, i, 0)),
        out_shape=jax.ShapeDtypeStruct((B, L, _W_AQ), jnp.bfloat16),
        scratch_shapes=[pltpu.VMEM((L + KC, tq), jnp.int32),
                        pltpu.VMEM((1, A_HEADS * tq), jnp.float32),
                        pltpu.VMEM((1, A_HEADS * tq), jnp.float32),
                        pltpu.VMEM((HEAD_DIM, A_HEADS * tq), jnp.float32),
                        pltpu.VMEM((_W_AQ, tq), jnp.float32)],
        compiler_params=_cparams(2),
        name="dsa",
    )(ik, ka, va, iq, qa, iw)


def _dilated_kernel(q_ref, kc_ref, kp_ref, vc_ref, vp_ref, o_ref, lse_ref):
    tqb = q_ref.shape[2]
    w = q_ref.shape[3]
    ib = pl.program_id(2)
    q = q_ref[0, 0]
    kcat = jnp.concatenate([kp_ref[0, 0], kc_ref[0, 0]], axis=0)
    vcat = jnp.concatenate([vp_ref[0, 0], vc_ref[0, 0]], axis=0)
    head_of_lane = lax.broadcasted_iota(jnp.int32, (1, w), 1) // HEAD_DIM
    rr = lax.broadcasted_iota(jnp.int32, (BLOCK, 2 * BLOCK), 0)
    cc = lax.broadcasted_iota(jnp.int32, (BLOCK, 2 * BLOCK), 1)
    band = (cc >= rr) & (cc <= rr + B_KEYS)
    for sb in range(tqb // BLOCK):
        qs = q[sb * BLOCK:(sb + 1) * BLOCK]
        ks = kcat[sb * BLOCK:sb * BLOCK + 2 * BLOCK]
        vs = vcat[sb * BLOCK:sb * BLOCK + 2 * BLOCK]
        mask = band
        if sb == 0:
            mask = band & ((cc >= BLOCK) | (ib > 0))
        o_acc = jnp.zeros((BLOCK, w), jnp.float32)
        lse_acc = jnp.zeros((BLOCK, w), jnp.float32)
        for hh in range(B_HPG):
            hm = head_of_lane == hh
            qm = jnp.where(hm, qs, jnp.zeros_like(qs))
            lg = lax.dot_general(qm, ks, (((1,), (1,)), ((), ())), preferred_element_type=jnp.float32)
            lg = jnp.where(mask, lg, -jnp.inf)
            m = jnp.max(lg, axis=-1, keepdims=True)
            e = jnp.exp(lg - m)
            den = jnp.sum(e, axis=-1, keepdims=True)
            pv = jnp.dot(e.astype(vs.dtype), vs, preferred_element_type=jnp.float32)
            o_acc = jnp.where(hm, pv * (1.0 / den), o_acc)
            lse_acc = jnp.where(hm, m + jnp.log(den), lse_acc)
        o_ref[0, 0, sb * BLOCK:(sb + 1) * BLOCK, :] = o_acc
        lse_ref[0, 0, sb * BLOCK:(sb + 1) * BLOCK, :] = lse_acc


def _dilated_group(bq, bk, bv):
    B, dil, ld, w = bq.shape
    tqb = min(512, ld)
    per = tqb // BLOCK
    cur = pl.BlockSpec((1, 1, tqb, w), lambda b, r, i: (b, r, i, 0))
    prev = pl.BlockSpec((1, 1, BLOCK, w), lambda b, r, i: (b, r, jnp.maximum(i * per - 1, 0), 0))
    return pl.pallas_call(
        _dilated_kernel,
        grid=(B, dil, ld // tqb),
        in_specs=[cur, cur, prev, cur, prev],
        out_specs=(cur, cur),
        out_shape=(jax.ShapeDtypeStruct((B, dil, ld, w), jnp.float32),) * 2,
        compiler_params=_cparams(3),
        name=f"dilated{dil}",
    )(bq, bk, bk, bv, bv)


def _sinkwin_kernel(sink_ref, q_ref, kc_ref, kp_ref, vc_ref, vp_ref, o_ref):
    tqb = q_ref.shape[1]
    ib = pl.program_id(1)
    q = q_ref[0]
    kcat = jnp.concatenate([kp_ref[0], kc_ref[0]], axis=0)
    vcat = jnp.concatenate([vp_ref[0], vc_ref[0]], axis=0)
    kv_of_lane = lax.broadcasted_iota(jnp.int32, (1, LANES), 1) // HEAD_DIM
    rr = lax.broadcasted_iota(jnp.int32, (BLOCK, 2 * BLOCK), 0)
    cc = lax.broadcasted_iota(jnp.int32, (BLOCK, 2 * BLOCK), 1)
    band = (cc > rr) & (cc <= rr + C_WINDOW)
    for sb in range(tqb // BLOCK):
        ks = kcat[sb * BLOCK:sb * BLOCK + 2 * BLOCK]
        vs = vcat[sb * BLOCK:sb * BLOCK + 2 * BLOCK]
        mask = band
        if sb == 0:
            mask = band & ((cc >= BLOCK) | (ib > 0))
        for g in range(C_GRP):
            qs = q[sb * BLOCK:(sb + 1) * BLOCK, g * LANES:(g + 1) * LANES]
            o_acc = jnp.zeros((BLOCK, LANES), jnp.float32)
            for kk in range(C_KV_HEADS):
                hm = kv_of_lane == kk
                sink = sink_ref[kk * C_GRP + g]
                qm = jnp.where(hm, qs, jnp.zeros_like(qs))
                lg = lax.dot_general(qm, ks, (((1,), (1,)), ((), ())), preferred_element_type=jnp.float32)
                lg = jnp.where(mask, lg, -jnp.inf)
                m = jnp.maximum(jnp.max(lg, axis=-1, keepdims=True), sink)
                e = jnp.exp(lg - m)
                den = jnp.sum(e, axis=-1, keepdims=True) + jnp.exp(sink - m)
                pv = jnp.dot(e.astype(vs.dtype), vs, preferred_element_type=jnp.float32)
                o_acc = jnp.where(hm, pv * (1.0 / den), o_acc)
            o_ref[0, sb * BLOCK:(sb + 1) * BLOCK, g * LANES:(g + 1) * LANES] = o_acc.astype(o_ref.dtype)


def _sinkwin(cq, ck, cv, sinks):
    B, L, _ = cq.shape
    tqb = min(512, L)
    per = tqb // BLOCK
    cur = lambda w: pl.BlockSpec((1, tqb, w), lambda b, i: (b, i, 0))
    prev = lambda w: pl.BlockSpec((1, BLOCK, w), lambda b, i: (b, jnp.maximum(i * per - 1, 0), 0))
    return pl.pallas_call(
        _sinkwin_kernel,
        grid=(B, L // tqb),
        in_specs=[pl.BlockSpec(memory_space=pltpu.SMEM),
                  cur(_W_CQ), cur(_W_CKV), prev(_W_CKV), cur(_W_CKV), prev(_W_CKV)],
        out_specs=cur(_W_CQ),
        out_shape=jax.ShapeDtypeStruct((B, L, _W_CQ), jnp.bfloat16),
        compiler_params=_cparams(2),
        name="sinkwin",
    )(sinks, cq, ck, ck, cv, cv)


def _mixer_out_kernel(h_ref, g_ref, wg_ref, oa_ref, ob0_ref, ob1_ref, ob2_ref, l0_ref, l1_ref, l2_ref,
                      oc_ref, wa_ref, wb_ref, wc_ref, wo_ref, out_ref, *stage_refs):
    tm = h_ref.shape[1]
    h = h_ref[0]
    u = _rmsnorm_f32(h, g_ref[...]).astype(jnp.bfloat16)

    def token_major(ref, stage_ref):
        dil = ref.shape[1]
        for r in range(dil):
            for t in range(B_GW // LANES):
                stage_ref[t, pl.ds(r, tm // dil, stride=dil), :] = ref[0, r, :, t * LANES:(t + 1) * LANES]
        return jnp.concatenate([stage_ref[t] for t in range(B_GW // LANES)], axis=1)

    ob0, l0 = ob0_ref[0, 0], l0_ref[0, 0]
    ob1, l1 = token_major(ob1_ref, stage_refs[0]), token_major(l1_ref, stage_refs[1])
    ob2, l2 = token_major(ob2_ref, stage_refs[2]), token_major(l2_ref, stage_refs[3])
    mx = jnp.maximum(jnp.maximum(l0, l1), l2)
    e0, e1, e2 = jnp.exp(l0 - mx), jnp.exp(l1 - mx), jnp.exp(l2 - mx)
    ob = (e0 * ob0 + e1 * ob1 + e2 * ob2) * (1.0 / (e0 + e1 + e2))

    def gate(i):
        z = jnp.dot(u, wg_ref[:, i * D_MODEL:(i + 1) * D_MODEL], preferred_element_type=jnp.float32)
        return jax.nn.sigmoid(z)

    merged = gate(0) * jnp.dot(oa_ref[0], wa_ref[...], preferred_element_type=jnp.float32)
    merged = merged + gate(1) * jnp.dot(ob.astype(jnp.bfloat16), wb_ref[...], preferred_element_type=jnp.float32)
    merged = merged + gate(2) * jnp.dot(oc_ref[0], wc_ref[...], preferred_element_type=jnp.float32)
    out_ref[0] = h + jnp.dot(merged.astype(jnp.bfloat16), wo_ref[...], preferred_element_type=jnp.float32)


def _mixer_out(h, g, wg, oa, obs, lses, oc, wa, wb, wc, wo):
    B, L, D = h.shape
    tm = TM
    tok = lambda w: pl.BlockSpec((1, tm, w), lambda b, i: (b, i, 0))
    full = lambda a: pl.BlockSpec(a.shape, lambda b, i: (0, 0))
    grp = lambda a: pl.BlockSpec((1, a.shape[1], tm // a.shape[1], B_GW), lambda b, i: (b, 0, i, 0))
    return pl.pallas_call(
        _mixer_out_kernel,
        grid=(B, L // tm),
        in_specs=[tok(D), full(g), full(wg), tok(_W_AQ)] + [grp(a) for a in obs] + [grp(a) for a in lses]
                 + [tok(_W_CQ), full(wa), full(wb), full(wc), full(wo)],
        out_specs=tok(D),
        out_shape=jax.ShapeDtypeStruct((B, L, D), jnp.float32),
        scratch_shapes=[pltpu.VMEM((B_GW // LANES, tm, LANES), jnp.float32)] * 4,
        compiler_params=_cparams(2),
        name="mixer_out",
    )(h, g, wg, oa, *obs, *lses, oc, wa, wb, wc, wo)


def _ffn_kernel(h_ref, g_ref, wup_ref, wdn_ref, gf_ref, out_ref, *, final, chunk):
    h = h_ref[...]
    u = _rmsnorm_f32(h, g_ref[...]).astype(jnp.bfloat16)
    acc = h
    for c in range(D_FF // chunk):
        up = jnp.dot(u, wup_ref[:, c * chunk:(c + 1) * chunk], preferred_element_type=jnp.float32)
        act = jnp.square(jnp.maximum(up, 0.0)).astype(jnp.bfloat16)
        acc = acc + jnp.dot(act, wdn_ref[c * chunk:(c + 1) * chunk, :], preferred_element_type=jnp.float32)
    if final:
        acc = _rmsnorm_f32(acc, gf_ref[...])
    out_ref[...] = acc


def _ffn(h2d, g, wup, wdn, gf, final):
    N, D = h2d.shape
    tm = TM
    row = pl.BlockSpec((tm, D), lambda i: (i, 0))
    full = lambda a: pl.BlockSpec(a.shape, lambda i: (0, 0))
    return pl.pallas_call(
        functools.partial(_ffn_kernel, final=final, chunk=1024),
        grid=(N // tm,),
        in_specs=[row, full(g), full(wup), full(wdn), full(gf)],
        out_specs=row,
        out_shape=jax.ShapeDtypeStruct((N, D), jnp.float32),
        compiler_params=_cparams(1),
        name="ffn",
    )(h2d, g, wup, wdn, gf)


def _rope_tables(positions):
    inv = ROPE_THETA ** (-jnp.arange(HALF, dtype=jnp.float32) / HALF)
    ang = positions.astype(jnp.float32)[..., None] * inv
    c, s = jnp.cos(ang), jnp.sin(ang)
    reps = LANES // HEAD_DIM
    cos = jnp.tile(jnp.concatenate([c, c], axis=-1), (1, 1, reps))
    sin = jnp.tile(jnp.concatenate([-s, s], axis=-1), (1, 1, reps))
    return cos, sin, jnp.swapaxes(c, 1, 2), jnp.swapaxes(s, 1, 2)


def _layer_weights(w_in, idx_k_norm, w_c):
    col = lambda n: w_in[:, _OFF[n][0]:_OFF[n][1]]
    scale = HEAD_DIM ** -0.5
    idx_scale = IDX_DIM ** -0.5
    cq = col("cq").reshape(D_MODEL, C_KV_HEADS, C_GRP, HEAD_DIM).transpose(0, 2, 1, 3).reshape(D_MODEL, _W_CQ)
    wt = jnp.concatenate([col("ak"), col("ik"), col("bq") * scale, col("bk"), col("bv"),
                          cq * scale, col("ck"), col("cv")], axis=1).astype(jnp.bfloat16)
    wT = jnp.concatenate([col("aq") * scale, col("iq") * idx_scale, col("av"), col("iw")],
                         axis=1).T.astype(jnp.bfloat16)
    wg = col("gates").astype(jnp.bfloat16)
    gk = jnp.concatenate([jnp.ones((IDX_DIM,), jnp.float32), idx_k_norm])[None, :]
    wc = w_c.reshape(C_KV_HEADS, C_GRP, HEAD_DIM, D_MODEL).transpose(1, 0, 2, 3).reshape(_W_CQ, D_MODEL)
    return wt, wT, wg, gk, wc.astype(jnp.bfloat16)


def kernel(x, positions, attn_norm, w_in, idx_k_norm, sinks, w_a, w_b, w_c, w_o, mlp_norm, w_up, w_down, final_norm):
    B, L, D = x.shape
    depth = w_in.shape[0]
    assert D == D_MODEL and L % (16 * BLOCK) == 0
    tabs = _rope_tables(positions)
    bf = jnp.bfloat16
    h = x
    for l in range(depth):
        wt, wT, wg, gk, wc = _layer_weights(w_in[l], idx_k_norm[l], w_c[l])
        (ka, ik, bq0, bq1, bq2, bk0, bk1, bk2, bv0, bv1, bv2, cq, ck, cv, qa, iq, va, iw) = _inproj(
            h, attn_norm[l][None, :], wt, wT, tabs, gk)
        oa = _dsa(ik, ka, va, iq, qa, iw)
        obs, lses = [], []
        for q_, k_, v_ in ((bq0, bk0, bv0), (bq1, bk1, bv1), (bq2, bk2, bv2)):
            o, lse = _dilated_group(q_, k_, v_)
            obs.append(o)
            lses.append(lse)
        oc = _sinkwin(cq, ck, cv, sinks[l])
        h2 = _mixer_out(h, attn_norm[l][None, :], wg, oa, obs, lses, oc,
                        w_a[l].astype(bf), w_b[l].astype(bf), wc, w_o[l].astype(bf))
        h3 = _ffn(h2.reshape(B * L, D), mlp_norm[l][None, :], w_up[l].astype(bf), w_down[l].astype(bf),
                  final_norm[None, :], l == depth - 1)
        h = h3.reshape(B, L, D)
    return h
```

```python
import functools

import jax
import jax.numpy as jnp
from jax import lax
from jax.experimental import pallas as pl
from jax.experimental.pallas import tpu as pltpu

D_MODEL = 1024
HEAD_DIM = 64
HALF = HEAD_DIM // 2
ROPE_THETA = 10000.0
NORM_EPS = 1e-6
A_HEADS = 6
IDX_HEADS = 8
IDX_DIM = 64
TOPK_MAX = 256
B_GROUPS = ((128, 1), (512, 4), (2048, 16))
B_HPG = 4
B_HEADS = B_HPG * len(B_GROUPS)
B_GW = B_HPG * HEAD_DIM
B_KEYS = 128
C_HEADS = 8
C_KV_HEADS = 2
C_GRP = C_HEADS // C_KV_HEADS
C_WINDOW = 128
N_BRANCHES = 3
D_FF = 4 * D_MODEL

BLOCK = 128
LANES = 128
INT_MIN = -2 ** 31
I16_MIN = -2 ** 15
VMEM_LIMIT = 56 * 1024 * 1024

_W_AQ = A_HEADS * HEAD_DIM
_W_IQ = IDX_HEADS * IDX_DIM
_W_B = B_HEADS * HEAD_DIM
_W_CQ = C_HEADS * HEAD_DIM
_W_CKV = C_KV_HEADS * HEAD_DIM
_OFF = {}
_o = 0
for _name, _w in (("aq", _W_AQ), ("ak", HEAD_DIM), ("av", HEAD_DIM), ("iq", _W_IQ), ("ik", IDX_DIM),
                  ("iw", IDX_HEADS), ("bq", _W_B), ("bk", _W_B), ("bv", _W_B), ("cq", _W_CQ),
                  ("ck", _W_CKV), ("cv", _W_CKV), ("gates", N_BRANCHES * D_MODEL)):
    _OFF[_name] = (_o, _o + _w)
    _o += _w

KC = 256
TM = 512


def _rmsnorm_f32(x, g):
    return x * lax.rsqrt(jnp.mean(x * x, axis=-1, keepdims=True) + NORM_EPS) * g


def _cparams(n_axes):
    return pltpu.CompilerParams(dimension_semantics=("arbitrary",) * n_axes, vmem_limit_bytes=VMEM_LIMIT)


def _inproj_kernel(x_ref, g_ref, wt_ref, wT_ref, cos_ref, sin_ref, cosT_ref, sinT_ref, gk_ref,
                   ka_ref, ik_ref,
                   bq0_ref, bq1_ref, bq2_ref, bk0_ref, bk1_ref, bk2_ref, bv0_ref, bv1_ref, bv2_ref,
                   cq_ref, ck_ref, cv_ref, qa_ref, iq_ref, va_ref, iw_ref, stage_ref):
    tm = x_ref.shape[1]
    u = _rmsnorm_f32(x_ref[0], g_ref[...]).astype(jnp.bfloat16)
    cos = cos_ref[0]
    sin = sin_ref[0]
    lane = lax.broadcasted_iota(jnp.int32, (1, LANES), 1)
    first_half = (lane % HEAD_DIM) < HALF

    def rope_tile(z):
        rot = jnp.where(first_half, pltpu.roll(z, LANES - HALF, 1), pltpu.roll(z, HALF, 1))
        return z * cos + rot * sin

    def project(c0, width):
        return jnp.dot(u, wt_ref[:, c0:c0 + width], preferred_element_type=jnp.float32)

    c0 = 0
    z = project(c0, 2 * HEAD_DIM)
    is_ik = lane >= HEAD_DIM
    ms = jnp.sum(jnp.where(is_ik, z * z, 0.0), axis=-1, keepdims=True) * (1.0 / IDX_DIM)
    z = z * jnp.where(is_ik, lax.rsqrt(ms + NORM_EPS) * gk_ref[...], 1.0)
    z = rope_tile(z)
    ka_ref[0] = z[:, :HEAD_DIM].astype(ka_ref.dtype)
    ik_ref[0] = z[:, HEAD_DIM:].astype(ik_ref.dtype)
    c0 += 2 * HEAD_DIM

    for refs, roped in (((bq0_ref, bq1_ref, bq2_ref), True), ((bk0_ref, bk1_ref, bk2_ref), True),
                        ((bv0_ref, bv1_ref, bv2_ref), False)):
        for g, (ref, (_, dil)) in enumerate(zip(refs, B_GROUPS)):
            z = project(c0 + g * B_GW, B_GW)
            if roped:
                z = jnp.concatenate([rope_tile(z[:, t * LANES:(t + 1) * LANES]) for t in range(B_GW // LANES)], axis=1)
            if dil == 1:
                ref[0, 0] = z.astype(ref.dtype)
            else:
                for t in range(B_GW // LANES):
                    stage_ref[t] = z[:, t * LANES:(t + 1) * LANES]
                for r in range(dil):
                    for t in range(B_GW // LANES):
                        ref[0, r, :, t * LANES:(t + 1) * LANES] = (
                            stage_ref[t, pl.ds(r, tm // dil, stride=dil), :].astype(ref.dtype))
        c0 += _W_B

    for ref, width, roped in ((cq_ref, _W_CQ, True), (ck_ref, _W_CKV, True), (cv_ref, _W_CKV, False)):
        z = project(c0, width)
        if roped:
            for t in range(width // LANES):
                ref[0, :, t * LANES:(t + 1) * LANES] = rope_tile(z[:, t * LANES:(t + 1) * LANES]).astype(ref.dtype)
        else:
            ref[0] = z.astype(ref.dtype)
        c0 += width

    def project_t(r0, rows):
        return lax.dot_general(wT_ref[r0:r0 + rows, :], u, (((1,), (1,)), ((), ())),
                               preferred_element_type=jnp.float32)

    cosT = cosT_ref[0]
    sinT = sinT_ref[0]

    def rope_rows(zt, ref, heads):
        for h in range(heads):
            x1 = zt[h * HEAD_DIM:h * HEAD_DIM + HALF]
            x2 = zt[h * HEAD_DIM + HALF:(h + 1) * HEAD_DIM]
            o1 = (x1 * cosT - x2 * sinT).astype(ref.dtype)
            o2 = (x2 * cosT + x1 * sinT).astype(ref.dtype)
            for c in range(tm // KC):
                ref[0, c, 0:HALF, h * KC:(h + 1) * KC] = o1[:, c * KC:(c + 1) * KC]
                ref[0, c, HALF:HEAD_DIM, h * KC:(h + 1) * KC] = o2[:, c * KC:(c + 1) * KC]

    r0 = 0
    rope_rows(project_t(r0, _W_AQ), qa_ref, A_HEADS)
    r0 += _W_AQ
    rope_rows(project_t(r0, _W_IQ), iq_ref, IDX_HEADS)
    r0 += _W_IQ
    zv = project_t(r0, HEAD_DIM)
    for c in range(tm // KC):
        va_ref[0, c] = zv[:, c * KC:(c + 1) * KC].astype(va_ref.dtype)
    r0 += HEAD_DIM
    zw = project_t(r0, IDX_HEADS) * (IDX_HEADS ** -0.5)
    for c in range(tm // KC):
        for h in range(IDX_HEADS):
            iw_ref[0, c, :, h * KC:(h + 1) * KC] = zw[h:h + 1, c * KC:(c + 1) * KC]


def _inproj(h, g, wt, wT, tabs, gk):
    B, L, D = h.shape
    tm = TM
    nq = L // KC
    bf = jnp.bfloat16
    tok = lambda w: pl.BlockSpec((1, tm, w), lambda b, i: (b, i, 0))
    tr = lambda r: pl.BlockSpec((1, r, tm), lambda b, i: (b, 0, i))
    full2 = lambda a: pl.BlockSpec(a.shape, lambda b, i: (0, 0))
    qtile = lambda rows, w: pl.BlockSpec((1, tm // KC, rows, w), lambda b, i: (b, i, 0, 0))
    grp_shape = lambda dil: jax.ShapeDtypeStruct((B, dil, L // dil, B_GW), bf)
    grp_spec = lambda dil: pl.BlockSpec((1, dil, tm // dil, B_GW), lambda b, i: (b, 0, i, 0))
    dils = [d for _, d in B_GROUPS]
    out_shape = (
        [jax.ShapeDtypeStruct((B, L, HEAD_DIM), bf), jax.ShapeDtypeStruct((B, L, IDX_DIM), bf)]
        + [grp_shape(d) for d in dils] * 3
        + [jax.ShapeDtypeStruct((B, L, _W_CQ), bf), jax.ShapeDtypeStruct((B, L, _W_CKV), bf),
           jax.ShapeDtypeStruct((B, L, _W_CKV), bf),
           jax.ShapeDtypeStruct((B, nq, HEAD_DIM, A_HEADS * KC), bf),
           jax.ShapeDtypeStruct((B, nq, IDX_DIM, IDX_HEADS * KC), bf),
           jax.ShapeDtypeStruct((B, nq, HEAD_DIM, KC), bf),
           jax.ShapeDtypeStruct((B, nq, 1, IDX_HEADS * KC), jnp.float32)])
    out_specs = (
        [tok(HEAD_DIM), tok(IDX_DIM)] + [grp_spec(d) for d in dils] * 3
        + [tok(_W_CQ), tok(_W_CKV), tok(_W_CKV),
           qtile(HEAD_DIM, A_HEADS * KC), qtile(IDX_DIM, IDX_HEADS * KC), qtile(HEAD_DIM, KC),
           qtile(1, IDX_HEADS * KC)])
    cos, sin, cosT, sinT = tabs
    return pl.pallas_call(
        _inproj_kernel,
        grid=(B, L // tm),
        in_specs=[tok(D), full2(g), full2(wt), full2(wT), tok(LANES), tok(LANES), tr(HALF), tr(HALF), full2(gk)],
        out_specs=out_specs,
        out_shape=out_shape,
        scratch_shapes=[pltpu.VMEM((B_GW // LANES, tm, LANES), jnp.float32)],
        compiler_params=_cparams(2),
        name="inproj",
    )(h, g, wt, wT, cos, sin, cosT, sinT, gk)


def _dsa_kernel(ik_ref, ka_ref, va_ref, iq_ref, qa_ref, iw_ref, o_ref,
                key_ref, hi_ref, lo_ref, r_ref, lg_ref, cm_ref, p_ref, m_ref, l_ref, alpha_ref, acc_ref, outT_ref,
                *, topk):
    tq = o_ref.shape[1]
    qi = pl.program_id(1)
    nch = qi + 1
    npair = (nch + 1) // 2
    last = nch - 1
    col = qi * tq + lax.broadcasted_iota(jnp.int32, (KC, tq), 1)
    row0 = lax.broadcasted_iota(jnp.int32, (KC, tq), 0)
    f32 = jnp.float32

    def score_dot(j):
        s0 = pl.multiple_of(j * KC, KC)
        return jnp.dot(ik_ref[0, pl.ds(s0, KC), :], iq_ref[0, 0], preferred_element_type=f32)

    r_ref[0] = score_dot(0)

    def score_chunk(j, slot):
        s0 = pl.multiple_of(j * KC, KC)
        r_ref[1 - slot] = score_dot(jnp.minimum(j + 1, last))
        wr = jnp.maximum(r_ref[slot], 0.0) * iw_ref[0, 0]
        acc = wr[:, 0:tq]
        for h in range(1, IDX_HEADS):
            acc = acc + wr[:, h * tq:(h + 1) * tq]
        bits = lax.bitcast_convert_type(acc, jnp.int32)
        key = jnp.where(bits < 0, INT_MIN - bits, bits)
        key = jnp.where(row0 + s0 <= col, key, INT_MIN)
        key_ref[pl.ds(s0, KC), :] = key
        hi_ref[pl.ds(s0, KC), :] = lax.shift_right_arithmetic(key, 16).astype(jnp.int16)

    def score_pair(j2, carry):
        score_chunk(2 * j2, 0)
        score_chunk(jnp.minimum(2 * j2 + 1, last), 1)
        return carry

    lax.fori_loop(0, npair, score_pair, 0)

    @pl.when(nch % 2 == 1)
    def _():
        pad0 = pl.multiple_of(nch * KC, KC)
        hi_ref[pl.ds(pad0, KC), :] = jnp.full((KC, tq), I16_MIN, jnp.int16)
        lo_ref[pl.ds(pad0, KC), :] = jnp.full((KC, tq), I16_MIN, jnp.int16)

    one_b = jnp.ones((), jnp.bfloat16)
    zero_b = jnp.zeros((), jnp.bfloat16)
    rows = 2 * KC
    nacc = 4

    def count16(ref, cand16):
        def cnt_body(j, c):
            s0 = pl.multiple_of(j * rows, rows)
            w = jnp.where(ref[pl.ds(s0, rows), :] >= cand16, one_b, zero_b)
            accs = [w[a * 16:(a + 1) * 16] for a in range(nacc)]
            for i in range(nacc, rows // 16):
                accs[i % nacc] = accs[i % nacc] + w[i * 16:(i + 1) * 16]
            return c + ((accs[0] + accs[1]) + (accs[2] + accs[3])).astype(f32)
        c16 = lax.fori_loop(0, npair, cnt_body, jnp.zeros((16, tq), f32))
        return jnp.sum(c16, axis=0, keepdims=True)

    def bit_step(ref, k_want, shift, st):
        t_u, cnt, c_rej = st
        cand_u = t_u | lax.shift_left(jnp.int32(1), shift)
        c = count16(ref, (cand_u - 32768).astype(jnp.int16))
        take = c >= k_want
        return jnp.where(take, cand_u, t_u), jnp.where(take, c, cnt), jnp.where(take, c_rej, c)

    def search16(ref, k_want, cnt0, active, early_exit):
        st0 = (jnp.zeros((1, tq), jnp.int32), cnt0, jnp.zeros((1, tq), f32))
        if not early_exit:
            return lax.fori_loop(0, 16, lambda i, st: bit_step(ref, k_want, 15 - i, st), st0)

        def cond(st):
            g, (_, cnt, _) = st
            pending = jnp.logical_and(active, cnt != k_want)
            return jnp.logical_and(g < 4, jnp.max(pending.astype(f32)) > 0.0)

        def body(st):
            g, inner = st
            for k in range(4):
                inner = bit_step(ref, k_want, 15 - (4 * g + k), inner)
            return g + 1, inner

        return lax.while_loop(cond, body, (jnp.int32(0), st0))[1]

    n_valid = (qi * tq + lax.broadcasted_iota(jnp.int32, (1, tq), 1) + 1).astype(f32)
    kf = jnp.full((1, tq), float(topk), f32)
    active = n_valid > kf
    th_u, cnt_h, gt_h = search16(hi_ref, kf, n_valid, active, early_exit=False)
    th16 = (th_u - 32768).astype(jnp.int16)

    def lo_body(j, carry):
        s0 = pl.multiple_of(j * KC, KC)
        lo = ((key_ref[pl.ds(s0, KC), :] & 0xFFFF) - 32768).astype(jnp.int16)
        lo_ref[pl.ds(s0, KC), :] = jnp.where(hi_ref[pl.ds(s0, KC), :] == th16, lo, jnp.full_like(lo, I16_MIN))
        return carry

    lax.fori_loop(0, nch, lo_body, 0)
    active_lo = jnp.logical_and(active, cnt_h != kf)
    k_lo = jnp.where(active_lo, kf - gt_h, jnp.inf)
    tl_u, cnt_l, gt_l = search16(lo_ref, k_lo, cnt_h - gt_h, active_lo, early_exit=True)
    t32 = lax.shift_left(th_u - 32768, 16) | tl_u
    thr = jnp.maximum(t32, INT_MIN + 1)

    tied = jnp.logical_and(active_lo, cnt_l > k_lo)

    @pl.when(jnp.max(tied.astype(f32)) > 0.0)
    def _():
        need = k_lo - gt_l
        tri = (lax.broadcasted_iota(jnp.int32, (KC, KC), 1)
               < lax.broadcasted_iota(jnp.int32, (KC, KC), 0)).astype(jnp.bfloat16)

        def tie_body(j, run):
            s0 = pl.multiple_of(j * KC, KC)
            k = key_ref[pl.ds(s0, KC), :]
            eq = k == thr
            before = run + jnp.dot(tri, eq.astype(jnp.bfloat16), preferred_element_type=f32)
            drop = jnp.logical_and(jnp.logical_and(eq, tied), before >= need)
            key_ref[pl.ds(s0, KC), :] = jnp.where(drop, INT_MIN, k)
            return run + jnp.sum(eq.astype(f32), axis=0, keepdims=True)

        lax.fori_loop(0, nch, tie_body, jnp.zeros((1, tq), f32))

    def logits_stage(j, slot, valid):
        s0 = pl.multiple_of(j * KC, KC)
        lg = jnp.dot(ka_ref[0, pl.ds(s0, KC), :], qa_ref[0, 0], preferred_element_type=f32)
        sel = jnp.logical_and(key_ref[pl.ds(s0, KC), :] >= thr, valid)
        bias = jnp.where(sel, 0.0, -jnp.inf)
        for h in range(A_HEADS):
            lgh = lg[:, h * tq:(h + 1) * tq] + bias
            lg_ref[slot, :, h * tq:(h + 1) * tq] = lgh
            cm_ref[slot, :, h * tq:(h + 1) * tq] = jnp.max(lgh, axis=0, keepdims=True)

    m_ref[...] = jnp.full(m_ref.shape, -1e30, f32)
    l_ref[...] = jnp.zeros(l_ref.shape, f32)
    alpha_ref[...] = jnp.ones(alpha_ref.shape, f32)
    acc_ref[...] = jnp.zeros(acc_ref.shape, f32)
    p_ref[1] = jnp.zeros(p_ref.shape[1:], p_ref.dtype)
    logits_stage(0, 0, True)

    def att_chunk(jnext, valid_next, jprev, slot):
        logits_stage(jnext, 1 - slot, valid_next)
        pv = jnp.dot(va_ref[0, jprev], p_ref[1 - slot], preferred_element_type=f32)
        acc_ref[...] = alpha_ref[...] * acc_ref[...] + pv
        m_old = m_ref[...]
        m_new = jnp.maximum(m_old, cm_ref[slot])
        sums = []
        for h in range(A_HEADS):
            ph = jnp.exp(lg_ref[slot, :, h * tq:(h + 1) * tq] - m_new[:, h * tq:(h + 1) * tq])
            p_ref[slot, :, h * tq:(h + 1) * tq] = ph.astype(p_ref.dtype)
            sums.append(jnp.sum(ph, axis=0, keepdims=True))
        alpha = jnp.exp(m_old - m_new)
        l_ref[...] = alpha * l_ref[...] + jnp.concatenate(sums, axis=1)
        alpha_ref[...] = alpha
        m_ref[...] = m_new

    def att_pair(j2, carry):
        ja = 2 * j2
        jb = jnp.minimum(ja + 1, last)
        att_chunk(jb, ja + 1 <= last, jnp.maximum(ja - 1, 0), 0)
        att_chunk(jnp.minimum(ja + 2, last), True, ja, 1)
        return carry

    lax.fori_loop(0, npair, att_pair, 0)
    pv = jnp.dot(va_ref[0, last], p_ref[1], preferred_element_type=f32)
    out = (alpha_ref[...] * acc_ref[...] + pv) * (1.0 / l_ref[...])
    for h in range(A_HEADS):
        outT_ref[h * HEAD_DIM:(h + 1) * HEAD_DIM, :] = out[:, h * tq:(h + 1) * tq]
    o_ref[0] = outT_ref[...].T.astype(o_ref.dtype)


def _dsa(ik, ka, va, iq, qa, iw):
    B, L, _ = ik.shape
    tq = KC
    topk = min(TOPK_MAX, L // 4)
    seq = lambda w: pl.BlockSpec((1, L, w), lambda b, i: (b, 0, 0))
    qtile = lambda rows, w: pl.BlockSpec((1, 1, rows, w), lambda b, i: (b, i, 0, 0))
    f32 = jnp.float32
    return pl.pallas_call(
        functools.partial(_dsa_kernel, topk=topk),
        grid=(B, L // tq),
        in_specs=[seq(IDX_DIM), seq(HEAD_DIM),
                  pl.BlockSpec((1, L // KC, HEAD_DIM, KC), lambda b, i: (b, 0, 0, 0)),
                  qtile(IDX_DIM, IDX_HEADS * tq), qtile(HEAD_DIM, A_HEADS * tq), qtile(1, IDX_HEADS * tq)],
        out_specs=pl.BlockSpec((1, tq, _W_AQ), lambda b, i: (b, i, 0)),
        out_shape=jax.ShapeDtypeStruct((B, L, _W_AQ), jnp.bfloat16),
        scratch_shapes=[pltpu.VMEM((L, tq), jnp.int32),
                        pltpu.VMEM((L + KC, tq), jnp.int16),
                        pltpu.VMEM((L + KC, tq), jnp.int16),
                        pltpu.VMEM((2, KC, IDX_HEADS * tq), f32),
                        pltpu.VMEM((2, KC, A_HEADS * tq), f32),
                        pltpu.VMEM((2, 1, A_HEADS * tq), f32),
                        pltpu.VMEM((2, KC, A_HEADS * tq), jnp.bfloat16),
                        pltpu.VMEM((1, A_HEADS * tq), f32),
                        pltpu.VMEM((1, A_HEADS * tq), f32),
                        pltpu.VMEM((1, A_HEADS * tq), f32),
                        pltpu.VMEM((HEAD_DIM, A_HEADS * tq), f32),
                        pltpu.VMEM((_W_AQ, tq), f32)],
        compiler_params=_cparams(2),
        name="dsa",
    )(ik, ka, va, iq, qa, iw)


def _dilated_kernel(q_ref, kc_ref, kp_ref, vc_ref, vp_ref, o_ref, lse_ref):
    tqb = q_ref.shape[2]
    w = q_ref.shape[3]
    ib = pl.program_id(2)
    q = q_ref[0, 0]
    kcat = jnp.concatenate([kp_ref[0, 0], kc_ref[0, 0]], axis=0)
    vcat = jnp.concatenate([vp_ref[0, 0], vc_ref[0, 0]], axis=0)
    head_of_lane = lax.broadcasted_iota(jnp.int32, (1, w), 1) // HEAD_DIM
    nrow = B_HPG * BLOCK
    rr = lax.broadcasted_iota(jnp.int32, (nrow, 2 * BLOCK), 0) % BLOCK
    cc = lax.broadcasted_iota(jnp.int32, (nrow, 2 * BLOCK), 1)
    band = (cc >= rr) & (cc <= rr + B_KEYS)
    for sb in range(tqb // BLOCK):
        qs = q[sb * BLOCK:(sb + 1) * BLOCK]
        ks = kcat[sb * BLOCK:sb * BLOCK + 2 * BLOCK]
        vs = vcat[sb * BLOCK:sb * BLOCK + 2 * BLOCK]
        mask = band
        if sb == 0:
            mask = band & ((cc >= BLOCK) | (ib > 0))
        q4 = jnp.concatenate([jnp.where(head_of_lane == hh, qs, jnp.zeros_like(qs)) for hh in range(B_HPG)], axis=0)
        lg = lax.dot_general(q4, ks, (((1,), (1,)), ((), ())), preferred_element_type=jnp.float32)
        lg = jnp.where(mask, lg, -jnp.inf)
        m = jnp.max(lg, axis=-1, keepdims=True)
        e = jnp.exp(lg - m)
        den = jnp.sum(e, axis=-1, keepdims=True)
        pv = jnp.dot(e.astype(vs.dtype), vs, preferred_element_type=jnp.float32) * (1.0 / den)
        lse = m + jnp.log(den)
        o_acc = pv[0:BLOCK]
        lse_acc = jnp.broadcast_to(lse[0:BLOCK], (BLOCK, w))
        for hh in range(1, B_HPG):
            hm = head_of_lane == hh
            o_acc = jnp.where(hm, pv[hh * BLOCK:(hh + 1) * BLOCK], o_acc)
            lse_acc = jnp.where(hm, lse[hh * BLOCK:(hh + 1) * BLOCK], lse_acc)
        o_ref[0, 0, sb * BLOCK:(sb + 1) * BLOCK, :] = o_acc
        lse_ref[0, 0, sb * BLOCK:(sb + 1) * BLOCK, :] = lse_acc


def _dilated_group(bq, bk, bv):
    B, dil, ld, w = bq.shape
    tqb = min(512, ld)
    per = tqb // BLOCK
    cur = pl.BlockSpec((1, 1, tqb, w), lambda b, r, i: (b, r, i, 0))
    prev = pl.BlockSpec((1, 1, BLOCK, w), lambda b, r, i: (b, r, jnp.maximum(i * per - 1, 0), 0))
    return pl.pallas_call(
        _dilated_kernel,
        grid=(B, dil, ld // tqb),
        in_specs=[cur, cur, prev, cur, prev],
        out_specs=(cur, cur),
        out_shape=(jax.ShapeDtypeStruct((B, dil, ld, w), jnp.float32),) * 2,
        compiler_params=_cparams(3),
        name=f"dilated{dil}",
    )(bq, bk, bk, bv, bv)


def _sinkwin_kernel(sink_ref, q_ref, kc_ref, kp_ref, vc_ref, vp_ref, o_ref):
    tqb = q_ref.shape[1]
    ib = pl.program_id(1)
    q = q_ref[0]
    kcat = jnp.concatenate([kp_ref[0], kc_ref[0]], axis=0)
    vcat = jnp.concatenate([vp_ref[0], vc_ref[0]], axis=0)
    kv_of_lane = lax.broadcasted_iota(jnp.int32, (1, LANES), 1) // HEAD_DIM
    rr = lax.broadcasted_iota(jnp.int32, (BLOCK, 2 * BLOCK), 0)
    cc = lax.broadcasted_iota(jnp.int32, (BLOCK, 2 * BLOCK), 1)
    band = (cc > rr) & (cc <= rr + C_WINDOW)
    for sb in range(tqb // BLOCK):
        ks = kcat[sb * BLOCK:sb * BLOCK + 2 * BLOCK]
        vs = vcat[sb * BLOCK:sb * BLOCK + 2 * BLOCK]
        mask = band
        if sb == 0:
            mask = band & ((cc >= BLOCK) | (ib > 0))
        for g in range(C_GRP):
            qs = q[sb * BLOCK:(sb + 1) * BLOCK, g * LANES:(g + 1) * LANES]
            o_acc = jnp.zeros((BLOCK, LANES), jnp.float32)
            for kk in range(C_KV_HEADS):
                hm = kv_of_lane == kk
                sink = sink_ref[kk * C_GRP + g]
                qm = jnp.where(hm, qs, jnp.zeros_like(qs))
                lg = lax.dot_general(qm, ks, (((1,), (1,)), ((), ())), preferred_element_type=jnp.float32)
                lg = jnp.where(mask, lg, -jnp.inf)
                m = jnp.maximum(jnp.max(lg, axis=-1, keepdims=True), sink)
                e = jnp.exp(lg - m)
                den = jnp.sum(e, axis=-1, keepdims=True) + jnp.exp(sink - m)
                pv = jnp.dot(e.astype(vs.dtype), vs, preferred_element_type=jnp.float32)
                o_acc = jnp.where(hm, pv * (1.0 / den), o_acc)
            o_ref[0, sb * BLOCK:(sb + 1) * BLOCK, g * LANES:(g + 1) * LANES] = o_acc.astype(o_ref.dtype)


def _sinkwin(cq, ck, cv, sinks):
    B, L, _ = cq.shape
    tqb = min(512, L)
    per = tqb // BLOCK
    cur = lambda w: pl.BlockSpec((1, tqb, w), lambda b, i: (b, i, 0))
    prev = lambda w: pl.BlockSpec((1, BLOCK, w), lambda b, i: (b, jnp.maximum(i * per - 1, 0), 0))
    return pl.pallas_call(
        _sinkwin_kernel,
        grid=(B, L // tqb),
        in_specs=[pl.BlockSpec(memory_space=pltpu.SMEM),
                  cur(_W_CQ), cur(_W_CKV), prev(_W_CKV), cur(_W_CKV), prev(_W_CKV)],
        out_specs=cur(_W_CQ),
        out_shape=jax.ShapeDtypeStruct((B, L, _W_CQ), jnp.bfloat16),
        compiler_params=_cparams(2),
        name="sinkwin",
    )(sinks, cq, ck, ck, cv, cv)


def _mixer_out_kernel(h_ref, g_ref, wg_ref, oa_ref, ob0_ref, ob1_ref, ob2_ref, l0_ref, l1_ref, l2_ref,
                      oc_ref, wa_ref, wb_ref, wc_ref, wo_ref, out_ref, *stage_refs):
    tm = h_ref.shape[1]
    h = h_ref[0]
    u = _rmsnorm_f32(h, g_ref[...]).astype(jnp.bfloat16)

    def token_major(ref, stage_ref):
        dil = ref.shape[1]
        for r in range(dil):
            for t in range(B_GW // LANES):
                stage_ref[t, pl.ds(r, tm // dil, stride=dil), :] = ref[0, r, :, t * LANES:(t + 1) * LANES]
        return jnp.concatenate([stage_ref[t] for t in range(B_GW // LANES)], axis=1)

    ob0, l0 = ob0_ref[0, 0], l0_ref[0, 0]
    ob1, l1 = token_major(ob1_ref, stage_refs[0]), token_major(l1_ref, stage_refs[1])
    ob2, l2 = token_major(ob2_ref, stage_refs[2]), token_major(l2_ref, stage_refs[3])
    mx = jnp.maximum(jnp.maximum(l0, l1), l2)
    e0, e1, e2 = jnp.exp(l0 - mx), jnp.exp(l1 - mx), jnp.exp(l2 - mx)
    ob = (e0 * ob0 + e1 * ob1 + e2 * ob2) * (1.0 / (e0 + e1 + e2))

    def gate(i):
        z = jnp.dot(u, wg_ref[:, i * D_MODEL:(i + 1) * D_MODEL], preferred_element_type=jnp.float32)
        return jax.nn.sigmoid(z)

    merged = gate(0) * jnp.dot(oa_ref[0], wa_ref[...], preferred_element_type=jnp.float32)
    merged = merged + gate(1) * jnp.dot(ob.astype(jnp.bfloat16), wb_ref[...], preferred_element_type=jnp.float32)
    merged = merged + gate(2) * jnp.dot(oc_ref[0], wc_ref[...], preferred_element_type=jnp.float32)
    out_ref[0] = h + jnp.dot(merged.astype(jnp.bfloat16), wo_ref[...], preferred_element_type=jnp.float32)


def _mixer_out(h, g, wg, oa, obs, lses, oc, wa, wb, wc, wo):
    B, L, D = h.shape
    tm = TM
    tok = lambda w: pl.BlockSpec((1, tm, w), lambda b, i: (b, i, 0))
    full = lambda a: pl.BlockSpec(a.shape, lambda b, i: (0, 0))
    grp = lambda a: pl.BlockSpec((1, a.shape[1], tm // a.shape[1], B_GW), lambda b, i: (b, 0, i, 0))
    return pl.pallas_call(
        _mixer_out_kernel,
        grid=(B, L // tm),
        in_specs=[tok(D), full(g), full(wg), tok(_W_AQ)] + [grp(a) for a in obs] + [grp(a) for a in lses]
                 + [tok(_W_CQ), full(wa), full(wb), full(wc), full(wo)],
        out_specs=tok(D),
        out_shape=jax.ShapeDtypeStruct((B, L, D), jnp.float32),
        scratch_shapes=[pltpu.VMEM((B_GW // LANES, tm, LANES), jnp.float32)] * 4,
        compiler_params=_cparams(2),
        name="mixer_out",
    )(h, g, wg, oa, *obs, *lses, oc, wa, wb, wc, wo)


def _ffn_kernel(h_ref, g_ref, wup_ref, wdn_ref, gf_ref, out_ref, *, final, chunk):
    h = h_ref[...]
    u = _rmsnorm_f32(h, g_ref[...]).astype(jnp.bfloat16)
    acc = h
    for c in range(D_FF // chunk):
        up = jnp.dot(u, wup_ref[:, c * chunk:(c + 1) * chunk], preferred_element_type=jnp.float32)
        act = jnp.square(jnp.maximum(up, 0.0)).astype(jnp.bfloat16)
        acc = acc + jnp.dot(act, wdn_ref[c * chunk:(c + 1) * chunk, :], preferred_element_type=jnp.float32)
    if final:
        acc = _rmsnorm_f32(acc, gf_ref[...])
    out_ref[...] = acc


def _ffn(h2d, g, wup, wdn, gf, final):
    N, D = h2d.shape
    tm = TM
    row = pl.BlockSpec((tm, D), lambda i: (i, 0))
    full = lambda a: pl.BlockSpec(a.shape, lambda i: (0, 0))
    return pl.pallas_call(
        functools.partial(_ffn_kernel, final=final, chunk=1024),
        grid=(N // tm,),
        in_specs=[row, full(g), full(wup), full(wdn), full(gf)],
        out_specs=row,
        out_shape=jax.ShapeDtypeStruct((N, D), jnp.float32),
        compiler_params=_cparams(1),
        name="ffn",
    )(h2d, g, wup, wdn, gf)


def _rope_tables(positions):
    inv = ROPE_THETA ** (-jnp.arange(HALF, dtype=jnp.float32) / HALF)
    ang = positions.astype(jnp.float32)[..., None] * inv
    c, s = jnp.cos(ang), jnp.sin(ang)
    reps = LANES // HEAD_DIM
    cos = jnp.tile(jnp.concatenate([c, c], axis=-1), (1, 1, reps))
    sin = jnp.tile(jnp.concatenate([-s, s], axis=-1), (1, 1, reps))
    return cos, sin, jnp.swapaxes(c, 1, 2), jnp.swapaxes(s, 1, 2)


def _layer_weights(w_in, idx_k_norm, w_c):
    col = lambda n: w_in[:, _OFF[n][0]:_OFF[n][1]]
    scale = HEAD_DIM ** -0.5
    idx_scale = IDX_DIM ** -0.5
    cq = col("cq").reshape(D_MODEL, C_KV_HEADS, C_GRP, HEAD_DIM).transpose(0, 2, 1, 3).reshape(D_MODEL, _W_CQ)
    wt = jnp.concatenate([col("ak"), col("ik"), col("bq") * scale, col("bk"), col("bv"),
                          cq * scale, col("ck"), col("cv")], axis=1).astype(jnp.bfloat16)
    wT = jnp.concatenate([col("aq") * scale, col("iq") * idx_scale, col("av"), col("iw")],
                         axis=1).T.astype(jnp.bfloat16)
    wg = col("gates").astype(jnp.bfloat16)
    gk = jnp.concatenate([jnp.ones((IDX_DIM,), jnp.float32), idx_k_norm])[None, :]
    wc = w_c.reshape(C_KV_HEADS, C_GRP, HEAD_DIM, D_MODEL).transpose(1, 0, 2, 3).reshape(_W_CQ, D_MODEL)
    return wt, wT, wg, gk, wc.astype(jnp.bfloat16)


def kernel(x, positions, attn_norm, w_in, idx_k_norm, sinks, w_a, w_b, w_c, w_o, mlp_norm, w_up, w_down, final_norm):
    B, L, D = x.shape
    depth = w_in.shape[0]
    assert D == D_MODEL and L % (16 * BLOCK) == 0
    tabs = _rope_tables(positions)
    bf = jnp.bfloat16
    h = x
    for l in range(depth):
        wt, wT, wg, gk, wc = _layer_weights(w_in[l], idx_k_norm[l], w_c[l])
        (ka, ik, bq0, bq1, bq2, bk0, bk1, bk2, bv0, bv1, bv2, cq, ck, cv, qa, iq, va, iw) = _inproj(
            h, attn_norm[l][None, :], wt, wT, tabs, gk)
        oa = _dsa(ik, ka, va, iq, qa, iw)
        obs, lses = [], []
        for q_, k_, v_ in ((bq0, bk0, bv0), (bq1, bk1, bv1), (bq2, bk2, bv2)):
            o, lse = _dilated_group(q_, k_, v_)
            obs.append(o)
            lses.append(lse)
        oc = _sinkwin(cq, ck, cv, sinks[l])
        h2 = _mixer_out(h, attn_norm[l][None, :], wg, oa, obs, lses, oc,
                        w_a[l].astype(bf), w_b[l].astype(bf), wc, w_o[l].astype(bf))
        h3 = _ffn(h2.reshape(B * L, D), mlp_norm[l][None, :], w_up[l].astype(bf), w_down[l].astype(bf),
                  final_norm[None, :], l == depth - 1)
        h = h3.reshape(B, L, D)
    return h
```

```python
import functools

import jax
import jax.numpy as jnp
from jax import lax
from jax.experimental import pallas as pl
from jax.experimental.pallas import tpu as pltpu

D_MODEL = 1024
HEAD_DIM = 64
HALF = HEAD_DIM // 2
ROPE_THETA = 10000.0
NORM_EPS = 1e-6
A_HEADS = 6
IDX_HEADS = 8
IDX_DIM = 64
TOPK_MAX = 256
B_GROUPS = ((128, 1), (512, 4), (2048, 16))
B_HPG = 4
B_HEADS = B_HPG * len(B_GROUPS)
B_GW = B_HPG * HEAD_DIM
B_KEYS = 128
C_HEADS = 8
C_KV_HEADS = 2
C_GRP = C_HEADS // C_KV_HEADS
C_WINDOW = 128
N_BRANCHES = 3
D_FF = 4 * D_MODEL

BLOCK = 128
LANES = 128
INT_MIN = -2 ** 31
I16_MIN = -2 ** 15
VMEM_LIMIT = 56 * 1024 * 1024

_W_AQ = A_HEADS * HEAD_DIM
_W_IQ = IDX_HEADS * IDX_DIM
_W_B = B_HEADS * HEAD_DIM
_W_CQ = C_HEADS * HEAD_DIM
_W_CKV = C_KV_HEADS * HEAD_DIM
_OFF = {}
_o = 0
for _name, _w in (("aq", _W_AQ), ("ak", HEAD_DIM), ("av", HEAD_DIM), ("iq", _W_IQ), ("ik", IDX_DIM),
                  ("iw", IDX_HEADS), ("bq", _W_B), ("bk", _W_B), ("bv", _W_B), ("cq", _W_CQ),
                  ("ck", _W_CKV), ("cv", _W_CKV), ("gates", N_BRANCHES * D_MODEL)):
    _OFF[_name] = (_o, _o + _w)
    _o += _w

KC = 256
DEN_ROWS = 16
LOG2E = 1.4426950408889634
TM = 512


def _rmsnorm_f32(x, g):
    return x * lax.rsqrt(jnp.mean(x * x, axis=-1, keepdims=True) + NORM_EPS) * g


def _cparams(n_axes):
    return pltpu.CompilerParams(dimension_semantics=("arbitrary",) * n_axes, vmem_limit_bytes=VMEM_LIMIT)


def _inproj_kernel(x_ref, g_ref, wt_ref, wT_ref, cos_ref, sin_ref, cosT_ref, sinT_ref, gk_ref,
                   ka_ref, ik_ref,
                   bq0_ref, bq1_ref, bq2_ref, bk0_ref, bk1_ref, bk2_ref, bv0_ref, bv1_ref, bv2_ref,
                   cq_ref, ck_ref, cv_ref, qa_ref, iq_ref, va_ref, iw_ref, stage_ref):
    tm = x_ref.shape[1]
    u = _rmsnorm_f32(x_ref[0], g_ref[...]).astype(jnp.bfloat16)
    cos = cos_ref[0]
    sin = sin_ref[0]
    lane = lax.broadcasted_iota(jnp.int32, (1, LANES), 1)
    first_half = (lane % HEAD_DIM) < HALF

    def rope_tile(z):
        rot = jnp.where(first_half, pltpu.roll(z, LANES - HALF, 1), pltpu.roll(z, HALF, 1))
        return z * cos + rot * sin

    def project(c0, width):
        return jnp.dot(u, wt_ref[:, c0:c0 + width], preferred_element_type=jnp.float32)

    c0 = 0
    z = project(c0, 2 * HEAD_DIM)
    is_ik = lane >= HEAD_DIM
    ms = jnp.sum(jnp.where(is_ik, z * z, 0.0), axis=-1, keepdims=True) * (1.0 / IDX_DIM)
    z = z * jnp.where(is_ik, lax.rsqrt(ms + NORM_EPS) * gk_ref[...], 1.0)
    z = rope_tile(z)
    ka_ref[0] = z[:, :HEAD_DIM].astype(ka_ref.dtype)
    ik_ref[0] = z[:, HEAD_DIM:].astype(ik_ref.dtype)
    c0 += 2 * HEAD_DIM

    for refs, roped in (((bq0_ref, bq1_ref, bq2_ref), True), ((bk0_ref, bk1_ref, bk2_ref), True),
                        ((bv0_ref, bv1_ref, bv2_ref), False)):
        for g, (ref, (_, dil)) in enumerate(zip(refs, B_GROUPS)):
            z = project(c0 + g * B_GW, B_GW)
            if roped:
                z = jnp.concatenate([rope_tile(z[:, t * LANES:(t + 1) * LANES]) for t in range(B_GW // LANES)], axis=1)
            if dil == 1:
                ref[0, 0] = z.astype(ref.dtype)
            else:
                for t in range(B_GW // LANES):
                    stage_ref[t] = z[:, t * LANES:(t + 1) * LANES]
                for r in range(dil):
                    for t in range(B_GW // LANES):
                        ref[0, r, :, t * LANES:(t + 1) * LANES] = (
                            stage_ref[t, pl.ds(r, tm // dil, stride=dil), :].astype(ref.dtype))
        c0 += _W_B

    for ref, width, roped in ((cq_ref, _W_CQ, True), (ck_ref, _W_CKV, True), (cv_ref, _W_CKV, False)):
        z = project(c0, width)
        if roped:
            for t in range(width // LANES):
                ref[0, :, t * LANES:(t + 1) * LANES] = rope_tile(z[:, t * LANES:(t + 1) * LANES]).astype(ref.dtype)
        else:
            ref[0] = z.astype(ref.dtype)
        c0 += width

    def project_t(r0, rows):
        return lax.dot_general(wT_ref[r0:r0 + rows, :], u, (((1,), (1,)), ((), ())),
                               preferred_element_type=jnp.float32)

    cosT = cosT_ref[0]
    sinT = sinT_ref[0]

    def rope_rows(zt, ref, heads):
        for h in range(heads):
            x1 = zt[h * HEAD_DIM:h * HEAD_DIM + HALF]
            x2 = zt[h * HEAD_DIM + HALF:(h + 1) * HEAD_DIM]
            o1 = (x1 * cosT - x2 * sinT).astype(ref.dtype)
            o2 = (x2 * cosT + x1 * sinT).astype(ref.dtype)
            for c in range(tm // KC):
                ref[0, c, 0:HALF, h * KC:(h + 1) * KC] = o1[:, c * KC:(c + 1) * KC]
                ref[0, c, HALF:HEAD_DIM, h * KC:(h + 1) * KC] = o2[:, c * KC:(c + 1) * KC]

    zt = project_t(0, wT_ref.shape[0])
    r0 = 0
    rope_rows(zt[r0:r0 + _W_AQ], qa_ref, A_HEADS)
    r0 += _W_AQ
    rope_rows(zt[r0:r0 + _W_IQ], iq_ref, IDX_HEADS)
    r0 += _W_IQ
    zv = zt[r0:r0 + HEAD_DIM]
    for c in range(tm // KC):
        va_ref[0, c] = zv[:, c * KC:(c + 1) * KC].astype(va_ref.dtype)
    r0 += HEAD_DIM
    zw = zt[r0:r0 + IDX_HEADS] * (IDX_HEADS ** -0.5)
    for c in range(tm // KC):
        for h in range(IDX_HEADS):
            iw_ref[0, c, :, h * KC:(h + 1) * KC] = zw[h:h + 1, c * KC:(c + 1) * KC]


def _inproj(h, g, wt, wT, tabs, gk):
    B, L, D = h.shape
    tm = TM
    nq = L // KC
    bf = jnp.bfloat16
    tok = lambda w: pl.BlockSpec((1, tm, w), lambda b, i: (b, i, 0))
    tr = lambda r: pl.BlockSpec((1, r, tm), lambda b, i: (b, 0, i))
    full2 = lambda a: pl.BlockSpec(a.shape, lambda b, i: (0, 0))
    qtile = lambda rows, w: pl.BlockSpec((1, tm // KC, rows, w), lambda b, i: (b, i, 0, 0))
    grp_shape = lambda dil: jax.ShapeDtypeStruct((B, dil, L // dil, B_GW), bf)
    grp_spec = lambda dil: pl.BlockSpec((1, dil, tm // dil, B_GW), lambda b, i: (b, 0, i, 0))
    dils = [d for _, d in B_GROUPS]
    out_shape = (
        [jax.ShapeDtypeStruct((B, L, HEAD_DIM), bf), jax.ShapeDtypeStruct((B, L, IDX_DIM), bf)]
        + [grp_shape(d) for d in dils] * 3
        + [jax.ShapeDtypeStruct((B, L, _W_CQ), bf), jax.ShapeDtypeStruct((B, L, _W_CKV), bf),
           jax.ShapeDtypeStruct((B, L, _W_CKV), bf),
           jax.ShapeDtypeStruct((B, nq, HEAD_DIM, A_HEADS * KC), bf),
           jax.ShapeDtypeStruct((B, nq, IDX_DIM, IDX_HEADS * KC), bf),
           jax.ShapeDtypeStruct((B, nq, HEAD_DIM, KC), bf),
           jax.ShapeDtypeStruct((B, nq, 1, IDX_HEADS * KC), jnp.float32)])
    out_specs = (
        [tok(HEAD_DIM), tok(IDX_DIM)] + [grp_spec(d) for d in dils] * 3
        + [tok(_W_CQ), tok(_W_CKV), tok(_W_CKV),
           qtile(HEAD_DIM, A_HEADS * KC), qtile(IDX_DIM, IDX_HEADS * KC), qtile(HEAD_DIM, KC),
           qtile(1, IDX_HEADS * KC)])
    cos, sin, cosT, sinT = tabs
    return pl.pallas_call(
        _inproj_kernel,
        grid=(B, L // tm),
        in_specs=[tok(D), full2(g), full2(wt), full2(wT), tok(LANES), tok(LANES), tr(HALF), tr(HALF), full2(gk)],
        out_specs=out_specs,
        out_shape=out_shape,
        scratch_shapes=[pltpu.VMEM((B_GW // LANES, tm, LANES), jnp.float32)],
        compiler_params=_cparams(2),
        name="inproj",
    )(h, g, wt, wT, cos, sin, cosT, sinT, gk)


def _dsa_kernel(ik_ref, ka_ref, va_ref, iq_ref, qa_ref, iw_ref, o_ref,
                key_ref, hi_ref, lo_ref, r_ref, lg_ref, cm_ref, p_ref, m_ref, alpha_ref, acc_ref, outT_ref,
                *, topk):
    tq = o_ref.shape[1]
    qi = pl.program_id(1)
    nch = qi + 1
    npair = (nch + 1) // 2
    last = nch - 1
    col = qi * tq + lax.broadcasted_iota(jnp.int32, (KC, tq), 1)
    row0 = lax.broadcasted_iota(jnp.int32, (KC, tq), 0)
    f32 = jnp.float32

    def score_dot(j):
        s0 = pl.multiple_of(j * KC, KC)
        return jnp.dot(ik_ref[0, pl.ds(s0, KC), :], iq_ref[0, 0], preferred_element_type=f32)

    r_ref[0] = score_dot(0)

    def score_chunk(j, slot):
        s0 = pl.multiple_of(j * KC, KC)
        r_ref[1 - slot] = score_dot(jnp.minimum(j + 1, last))
        wr = jnp.maximum(r_ref[slot], 0.0) * iw_ref[0, 0]
        acc = wr[:, 0:tq]
        for h in range(1, IDX_HEADS):
            acc = acc + wr[:, h * tq:(h + 1) * tq]
        bits = lax.bitcast_convert_type(acc, jnp.int32)
        key = jnp.where(bits < 0, INT_MIN - bits, bits)
        key = jnp.where(row0 + s0 <= col, key, INT_MIN)
        key_ref[pl.ds(s0, KC), :] = key
        hi_ref[pl.ds(s0, KC), :] = lax.shift_right_arithmetic(key, 16).astype(jnp.int16)

    def score_pair(j2, carry):
        score_chunk(2 * j2, 0)
        score_chunk(jnp.minimum(2 * j2 + 1, last), 1)
        return carry

    lax.fori_loop(0, npair, score_pair, 0)

    @pl.when(nch % 2 == 1)
    def _():
        pad0 = pl.multiple_of(nch * KC, KC)
        hi_ref[pl.ds(pad0, KC), :] = jnp.full((KC, tq), I16_MIN, jnp.int16)
        lo_ref[pl.ds(pad0, KC), :] = jnp.full((KC, tq), I16_MIN, jnp.int16)

    one_b = jnp.ones((), jnp.bfloat16)
    zero_b = jnp.zeros((), jnp.bfloat16)
    rows = 2 * KC
    nacc = 4

    def count16(ref, cand16):
        def cnt_body(j, c):
            s0 = pl.multiple_of(j * rows, rows)
            w = jnp.where(ref[pl.ds(s0, rows), :] >= cand16, one_b, zero_b)
            accs = [w[a * 16:(a + 1) * 16] for a in range(nacc)]
            for i in range(nacc, rows // 16):
                accs[i % nacc] = accs[i % nacc] + w[i * 16:(i + 1) * 16]
            return c + ((accs[0] + accs[1]) + (accs[2] + accs[3])).astype(f32)
        c16 = lax.fori_loop(0, npair, cnt_body, jnp.zeros((16, tq), f32))
        return jnp.sum(c16, axis=0, keepdims=True)

    def bit_step(ref, k_want, shift, st):
        t_u, cnt, c_rej = st
        cand_u = t_u | lax.shift_left(jnp.int32(1), shift)
        c = count16(ref, (cand_u - 32768).astype(jnp.int16))
        take = c >= k_want
        return jnp.where(take, cand_u, t_u), jnp.where(take, c, cnt), jnp.where(take, c_rej, c)

    def search16(ref, k_want, cnt0, active, early_exit):
        st0 = (jnp.zeros((1, tq), jnp.int32), cnt0, jnp.zeros((1, tq), f32))
        if not early_exit:
            return lax.fori_loop(0, 16, lambda i, st: bit_step(ref, k_want, 15 - i, st), st0)

        def cond(st):
            g, (_, cnt, _) = st
            pending = jnp.logical_and(active, cnt != k_want)
            return jnp.logical_and(g < 4, jnp.max(pending.astype(f32)) > 0.0)

        def body(st):
            g, inner = st
            for k in range(4):
                inner = bit_step(ref, k_want, 15 - (4 * g + k), inner)
            return g + 1, inner

        return lax.while_loop(cond, body, (jnp.int32(0), st0))[1]

    n_valid = (qi * tq + lax.broadcasted_iota(jnp.int32, (1, tq), 1) + 1).astype(f32)
    kf = jnp.full((1, tq), float(topk), f32)
    active = n_valid > kf
    th_u, cnt_h, gt_h = search16(hi_ref, kf, n_valid, active, early_exit=False)
    th16 = (th_u - 32768).astype(jnp.int16)

    def lo_body(j, carry):
        s0 = pl.multiple_of(j * KC, KC)
        lo = ((key_ref[pl.ds(s0, KC), :] & 0xFFFF) - 32768).astype(jnp.int16)
        lo_ref[pl.ds(s0, KC), :] = jnp.where(hi_ref[pl.ds(s0, KC), :] == th16, lo, jnp.full_like(lo, I16_MIN))
        return carry

    lax.fori_loop(0, nch, lo_body, 0)
    active_lo = jnp.logical_and(active, cnt_h != kf)
    k_lo = jnp.where(active_lo, kf - gt_h, jnp.inf)
    tl_u, cnt_l, gt_l = search16(lo_ref, k_lo, cnt_h - gt_h, active_lo, early_exit=True)
    t32 = lax.shift_left(th_u - 32768, 16) | tl_u
    thr = jnp.maximum(t32, INT_MIN + 1)

    tied = jnp.logical_and(active_lo, cnt_l > k_lo)

    @pl.when(jnp.max(tied.astype(f32)) > 0.0)
    def _():
        need = k_lo - gt_l
        tri = (lax.broadcasted_iota(jnp.int32, (KC, KC), 1)
               < lax.broadcasted_iota(jnp.int32, (KC, KC), 0)).astype(jnp.bfloat16)

        def tie_body(j, run):
            s0 = pl.multiple_of(j * KC, KC)
            k = key_ref[pl.ds(s0, KC), :]
            eq = k == thr
            before = run + jnp.dot(tri, eq.astype(jnp.bfloat16), preferred_element_type=f32)
            drop = jnp.logical_and(jnp.logical_and(eq, tied), before >= need)
            key_ref[pl.ds(s0, KC), :] = jnp.where(drop, INT_MIN, k)
            return run + jnp.sum(eq.astype(f32), axis=0, keepdims=True)

        lax.fori_loop(0, nch, tie_body, jnp.zeros((1, tq), f32))

    def logits_stage(j, slot, valid):
        s0 = pl.multiple_of(j * KC, KC)
        lg = jnp.dot(ka_ref[0, pl.ds(s0, KC), :], qa_ref[0, 0], preferred_element_type=f32)
        sel = jnp.logical_and(key_ref[pl.ds(s0, KC), :] >= thr, valid)
        bias = jnp.where(sel, 0.0, -jnp.inf)
        for h in range(A_HEADS):
            lgh = lg[:, h * tq:(h + 1) * tq] + bias
            lg_ref[slot, :, h * tq:(h + 1) * tq] = lgh
            cm_ref[slot, :, h * tq:(h + 1) * tq] = jnp.max(lgh, axis=0, keepdims=True)

    m_ref[...] = jnp.full(m_ref.shape, -1e30, f32)
    alpha_ref[...] = jnp.ones(alpha_ref.shape, f32)
    acc_ref[...] = jnp.zeros(acc_ref.shape, f32)
    p_ref[1] = jnp.zeros(p_ref.shape[1:], p_ref.dtype)
    logits_stage(0, 0, True)
    ones_rows = (lax.broadcasted_iota(jnp.int32, (DEN_ROWS, KC), 0) == 0).astype(jnp.bfloat16)

    def value_product(j, slot):
        v_ext = jnp.concatenate([va_ref[0, j], ones_rows], axis=0)
        return jnp.dot(v_ext, p_ref[slot], preferred_element_type=f32)

    def att_chunk(jnext, valid_next, jprev, slot):
        logits_stage(jnext, 1 - slot, valid_next)
        acc_ref[...] = alpha_ref[...] * acc_ref[...] + value_product(jprev, 1 - slot)
        m_old = m_ref[...]
        m_new = jnp.maximum(m_old, cm_ref[slot])
        for h in range(A_HEADS):
            ph = jnp.exp2(lg_ref[slot, :, h * tq:(h + 1) * tq] - m_new[:, h * tq:(h + 1) * tq])
            p_ref[slot, :, h * tq:(h + 1) * tq] = ph.astype(p_ref.dtype)
        alpha_ref[...] = jnp.exp2(m_old - m_new)
        m_ref[...] = m_new

    def att_pair(j2, carry):
        ja = 2 * j2
        jb = jnp.minimum(ja + 1, last)
        att_chunk(jb, ja + 1 <= last, jnp.maximum(ja - 1, 0), 0)
        att_chunk(jnp.minimum(ja + 2, last), True, ja, 1)
        return carry

    lax.fori_loop(0, npair, att_pair, 0)
    acc = alpha_ref[...] * acc_ref[...] + value_product(last, 1)
    out = acc[0:HEAD_DIM] * (1.0 / acc[HEAD_DIM:HEAD_DIM + 1])
    for h in range(A_HEADS):
        outT_ref[h * HEAD_DIM:(h + 1) * HEAD_DIM, :] = out[:, h * tq:(h + 1) * tq]
    o_ref[0] = outT_ref[...].T.astype(o_ref.dtype)


def _dsa(ik, ka, va, iq, qa, iw):
    B, L, _ = ik.shape
    tq = KC
    topk = min(TOPK_MAX, L // 4)
    seq = lambda w: pl.BlockSpec((1, L, w), lambda b, i: (b, 0, 0))
    qtile = lambda rows, w: pl.BlockSpec((1, 1, rows, w), lambda b, i: (b, i, 0, 0))
    f32 = jnp.float32
    return pl.pallas_call(
        functools.partial(_dsa_kernel, topk=topk),
        grid=(B, L // tq),
        in_specs=[seq(IDX_DIM), seq(HEAD_DIM),
                  pl.BlockSpec((1, L // KC, HEAD_DIM, KC), lambda b, i: (b, 0, 0, 0)),
                  qtile(IDX_DIM, IDX_HEADS * tq), qtile(HEAD_DIM, A_HEADS * tq), qtile(1, IDX_HEADS * tq)],
        out_specs=pl.BlockSpec((1, tq, _W_AQ), lambda b, i: (b, i, 0)),
        out_shape=jax.ShapeDtypeStruct((B, L, _W_AQ), jnp.bfloat16),
        scratch_shapes=[pltpu.VMEM((L, tq), jnp.int32),
                        pltpu.VMEM((L + KC, tq), jnp.int16),
                        pltpu.VMEM((L + KC, tq), jnp.int16),
                        pltpu.VMEM((2, KC, IDX_HEADS * tq), f32),
                        pltpu.VMEM((2, KC, A_HEADS * tq), f32),
                        pltpu.VMEM((2, 1, A_HEADS * tq), f32),
                        pltpu.VMEM((2, KC, A_HEADS * tq), jnp.bfloat16),
                        pltpu.VMEM((1, A_HEADS * tq), f32),
                        pltpu.VMEM((1, A_HEADS * tq), f32),
                        pltpu.VMEM((HEAD_DIM + DEN_ROWS, A_HEADS * tq), f32),
                        pltpu.VMEM((_W_AQ, tq), f32)],
        compiler_params=_cparams(2),
        name="dsa",
    )(ik, ka, va, iq, qa, iw)


def _dilated_kernel(q_ref, kc_ref, kp_ref, vc_ref, vp_ref, o_ref, lse_ref):
    tqb = q_ref.shape[2]
    w = q_ref.shape[3]
    ib = pl.program_id(2)
    q = q_ref[0, 0]
    kcat = jnp.concatenate([kp_ref[0, 0], kc_ref[0, 0]], axis=0)
    vcat = jnp.concatenate([vp_ref[0, 0], vc_ref[0, 0]], axis=0)
    head_of_lane = lax.broadcasted_iota(jnp.int32, (1, w), 1) // HEAD_DIM
    nrow = B_HPG * BLOCK
    rr = lax.broadcasted_iota(jnp.int32, (nrow, 2 * BLOCK), 0) % BLOCK
    cc = lax.broadcasted_iota(jnp.int32, (nrow, 2 * BLOCK), 1)
    band = (cc >= rr) & (cc <= rr + B_KEYS)
    for sb in range(tqb // BLOCK):
        qs = q[sb * BLOCK:(sb + 1) * BLOCK]
        ks = kcat[sb * BLOCK:sb * BLOCK + 2 * BLOCK]
        vs = vcat[sb * BLOCK:sb * BLOCK + 2 * BLOCK]
        mask = band
        if sb == 0:
            mask = band & ((cc >= BLOCK) | (ib > 0))
        q4 = jnp.concatenate([jnp.where(head_of_lane == hh, qs, jnp.zeros_like(qs)) for hh in range(B_HPG)], axis=0)
        lg = lax.dot_general(q4, ks, (((1,), (1,)), ((), ())), preferred_element_type=jnp.float32)
        lg = jnp.where(mask, lg, -jnp.inf)
        m = jnp.max(lg, axis=-1, keepdims=True)
        e = jnp.exp(lg - m)
        den = jnp.sum(e, axis=-1, keepdims=True)
        pv = jnp.dot(e.astype(vs.dtype), vs, preferred_element_type=jnp.float32) * (1.0 / den)
        lse = m + jnp.log(den)
        o_acc = pv[0:BLOCK]
        lse_acc = jnp.broadcast_to(lse[0:BLOCK], (BLOCK, w))
        for hh in range(1, B_HPG):
            hm = head_of_lane == hh
            o_acc = jnp.where(hm, pv[hh * BLOCK:(hh + 1) * BLOCK], o_acc)
            lse_acc = jnp.where(hm, lse[hh * BLOCK:(hh + 1) * BLOCK], lse_acc)
        o_ref[0, 0, sb * BLOCK:(sb + 1) * BLOCK, :] = o_acc
        lse_ref[0, 0, sb * BLOCK:(sb + 1) * BLOCK, :] = lse_acc


def _dilated_group(bq, bk, bv):
    B, dil, ld, w = bq.shape
    tqb = min(512, ld)
    per = tqb // BLOCK
    cur = pl.BlockSpec((1, 1, tqb, w), lambda b, r, i: (b, r, i, 0))
    prev = pl.BlockSpec((1, 1, BLOCK, w), lambda b, r, i: (b, r, jnp.maximum(i * per - 1, 0), 0))
    return pl.pallas_call(
        _dilated_kernel,
        grid=(B, dil, ld // tqb),
        in_specs=[cur, cur, prev, cur, prev],
        out_specs=(cur, cur),
        out_shape=(jax.ShapeDtypeStruct((B, dil, ld, w), jnp.float32),) * 2,
        compiler_params=_cparams(3),
        name=f"dilated{dil}",
    )(bq, bk, bk, bv, bv)


def _sinkwin_kernel(sink_ref, q_ref, kc_ref, kp_ref, vc_ref, vp_ref, o_ref):
    tqb = q_ref.shape[1]
    ib = pl.program_id(1)
    q = q_ref[0]
    kcat = jnp.concatenate([kp_ref[0], kc_ref[0]], axis=0)
    vcat = jnp.concatenate([vp_ref[0], vc_ref[0]], axis=0)
    kv_of_lane = lax.broadcasted_iota(jnp.int32, (1, LANES), 1) // HEAD_DIM
    rr = lax.broadcasted_iota(jnp.int32, (BLOCK, 2 * BLOCK), 0)
    cc = lax.broadcasted_iota(jnp.int32, (BLOCK, 2 * BLOCK), 1)
    band = (cc > rr) & (cc <= rr + C_WINDOW)
    for sb in range(tqb // BLOCK):
        ks = kcat[sb * BLOCK:sb * BLOCK + 2 * BLOCK]
        vs = vcat[sb * BLOCK:sb * BLOCK + 2 * BLOCK]
        mask = band
        if sb == 0:
            mask = band & ((cc >= BLOCK) | (ib > 0))
        for g in range(C_GRP):
            qs = q[sb * BLOCK:(sb + 1) * BLOCK, g * LANES:(g + 1) * LANES]
            o_acc = jnp.zeros((BLOCK, LANES), jnp.float32)
            for kk in range(C_KV_HEADS):
                hm = kv_of_lane == kk
                sink = sink_ref[kk * C_GRP + g]
                qm = jnp.where(hm, qs, jnp.zeros_like(qs))
                lg = lax.dot_general(qm, ks, (((1,), (1,)), ((), ())), preferred_element_type=jnp.float32)
                lg = jnp.where(mask, lg, -jnp.inf)
                m = jnp.maximum(jnp.max(lg, axis=-1, keepdims=True), sink)
                e = jnp.exp(lg - m)
                den = jnp.sum(e, axis=-1, keepdims=True) + jnp.exp(sink - m)
                pv = jnp.dot(e.astype(vs.dtype), vs, preferred_element_type=jnp.float32)
                o_acc = jnp.where(hm, pv * (1.0 / den), o_acc)
            o_ref[0, sb * BLOCK:(sb + 1) * BLOCK, g * LANES:(g + 1) * LANES] = o_acc.astype(o_ref.dtype)


def _sinkwin(cq, ck, cv, sinks):
    B, L, _ = cq.shape
    tqb = min(512, L)
    per = tqb // BLOCK
    cur = lambda w: pl.BlockSpec((1, tqb, w), lambda b, i: (b, i, 0))
    prev = lambda w: pl.BlockSpec((1, BLOCK, w), lambda b, i: (b, jnp.maximum(i * per - 1, 0), 0))
    return pl.pallas_call(
        _sinkwin_kernel,
        grid=(B, L // tqb),
        in_specs=[pl.BlockSpec(memory_space=pltpu.SMEM),
                  cur(_W_CQ), cur(_W_CKV), prev(_W_CKV), cur(_W_CKV), prev(_W_CKV)],
        out_specs=cur(_W_CQ),
        out_shape=jax.ShapeDtypeStruct((B, L, _W_CQ), jnp.bfloat16),
        compiler_params=_cparams(2),
        name="sinkwin",
    )(sinks, cq, ck, ck, cv, cv)


def _mixer_out_kernel(h_ref, g_ref, wg_ref, oa_ref, ob0_ref, ob1_ref, ob2_ref, l0_ref, l1_ref, l2_ref,
                      oc_ref, wa_ref, wb_ref, wc_ref, wo_ref, out_ref, *stage_refs):
    tm = h_ref.shape[1]
    h = h_ref[0]
    u = _rmsnorm_f32(h, g_ref[...]).astype(jnp.bfloat16)

    def token_major(ref, stage_ref):
        dil = ref.shape[1]
        for r in range(dil):
            for t in range(B_GW // LANES):
                stage_ref[t, pl.ds(r, tm // dil, stride=dil), :] = ref[0, r, :, t * LANES:(t + 1) * LANES]
        return jnp.concatenate([stage_ref[t] for t in range(B_GW // LANES)], axis=1)

    ob0, l0 = ob0_ref[0, 0], l0_ref[0, 0]
    ob1, l1 = token_major(ob1_ref, stage_refs[0]), token_major(l1_ref, stage_refs[1])
    ob2, l2 = token_major(ob2_ref, stage_refs[2]), token_major(l2_ref, stage_refs[3])
    mx = jnp.maximum(jnp.maximum(l0, l1), l2)
    e0, e1, e2 = jnp.exp(l0 - mx), jnp.exp(l1 - mx), jnp.exp(l2 - mx)
    ob = (e0 * ob0 + e1 * ob1 + e2 * ob2) * (1.0 / (e0 + e1 + e2))

    def gate(i):
        z = jnp.dot(u, wg_ref[:, i * D_MODEL:(i + 1) * D_MODEL], preferred_element_type=jnp.float32)
        return jax.nn.sigmoid(z)

    merged = gate(0) * jnp.dot(oa_ref[0], wa_ref[...], preferred_element_type=jnp.float32)
    merged = merged + gate(1) * jnp.dot(ob.astype(jnp.bfloat16), wb_ref[...], preferred_element_type=jnp.float32)
    merged = merged + gate(2) * jnp.dot(oc_ref[0], wc_ref[...], preferred_element_type=jnp.float32)
    out_ref[0] = h + jnp.dot(merged.astype(jnp.bfloat16), wo_ref[...], preferred_element_type=jnp.float32)


def _mixer_out(h, g, wg, oa, obs, lses, oc, wa, wb, wc, wo):
    B, L, D = h.shape
    tm = TM
    tok = lambda w: pl.BlockSpec((1, tm, w), lambda b, i: (b, i, 0))
    full = lambda a: pl.BlockSpec(a.shape, lambda b, i: (0, 0))
    grp = lambda a: pl.BlockSpec((1, a.shape[1], tm // a.shape[1], B_GW), lambda b, i: (b, 0, i, 0))
    return pl.pallas_call(
        _mixer_out_kernel,
        grid=(B, L // tm),
        in_specs=[tok(D), full(g), full(wg), tok(_W_AQ)] + [grp(a) for a in obs] + [grp(a) for a in lses]
                 + [tok(_W_CQ), full(wa), full(wb), full(wc), full(wo)],
        out_specs=tok(D),
        out_shape=jax.ShapeDtypeStruct((B, L, D), jnp.float32),
        scratch_shapes=[pltpu.VMEM((B_GW // LANES, tm, LANES), jnp.float32)] * 4,
        compiler_params=_cparams(2),
        name="mixer_out",
    )(h, g, wg, oa, *obs, *lses, oc, wa, wb, wc, wo)


def _ffn_kernel(h_ref, g_ref, wup_ref, wdn_ref, gf_ref, out_ref, *, final, chunk):
    h = h_ref[...]
    u = _rmsnorm_f32(h, g_ref[...]).astype(jnp.bfloat16)
    acc = h
    for c in range(D_FF // chunk):
        up = jnp.dot(u, wup_ref[:, c * chunk:(c + 1) * chunk], preferred_element_type=jnp.float32)
        act = jnp.square(jnp.maximum(up, 0.0)).astype(jnp.bfloat16)
        acc = acc + jnp.dot(act, wdn_ref[c * chunk:(c + 1) * chunk, :], preferred_element_type=jnp.float32)
    if final:
        acc = _rmsnorm_f32(acc, gf_ref[...])
    out_ref[...] = acc


def _ffn(h2d, g, wup, wdn, gf, final):
    N, D = h2d.shape
    tm = TM
    row = pl.BlockSpec((tm, D), lambda i: (i, 0))
    full = lambda a: pl.BlockSpec(a.shape, lambda i: (0, 0))
    return pl.pallas_call(
        functools.partial(_ffn_kernel, final=final, chunk=1024),
        grid=(N // tm,),
        in_specs=[row, full(g), full(wup), full(wdn), full(gf)],
        out_specs=row,
        out_shape=jax.ShapeDtypeStruct((N, D), jnp.float32),
        compiler_params=_cparams(1),
        name="ffn",
    )(h2d, g, wup, wdn, gf)


def _rope_tables(positions):
    inv = ROPE_THETA ** (-jnp.arange(HALF, dtype=jnp.float32) / HALF)
    ang = positions.astype(jnp.float32)[..., None] * inv
    c, s = jnp.cos(ang), jnp.sin(ang)
    reps = LANES // HEAD_DIM
    cos = jnp.tile(jnp.concatenate([c, c], axis=-1), (1, 1, reps))
    sin = jnp.tile(jnp.concatenate([-s, s], axis=-1), (1, 1, reps))
    return cos, sin, jnp.swapaxes(c, 1, 2), jnp.swapaxes(s, 1, 2)


def _layer_weights(w_in, idx_k_norm, w_c):
    col = lambda n: w_in[:, _OFF[n][0]:_OFF[n][1]]
    scale = HEAD_DIM ** -0.5
    idx_scale = IDX_DIM ** -0.5
    cq = col("cq").reshape(D_MODEL, C_KV_HEADS, C_GRP, HEAD_DIM).transpose(0, 2, 1, 3).reshape(D_MODEL, _W_CQ)
    wt = jnp.concatenate([col("ak"), col("ik"), col("bq") * scale, col("bk"), col("bv"),
                          cq * scale, col("ck"), col("cv")], axis=1).astype(jnp.bfloat16)
    wT = jnp.concatenate([col("aq") * (scale * LOG2E), col("iq") * idx_scale, col("av"), col("iw")],
                         axis=1).T.astype(jnp.bfloat16)
    wg = col("gates").astype(jnp.bfloat16)
    gk = jnp.concatenate([jnp.ones((IDX_DIM,), jnp.float32), idx_k_norm])[None, :]
    wc = w_c.reshape(C_KV_HEADS, C_GRP, HEAD_DIM, D_MODEL).transpose(1, 0, 2, 3).reshape(_W_CQ, D_MODEL)
    return wt, wT, wg, gk, wc.astype(jnp.bfloat16)


def kernel(x, positions, attn_norm, w_in, idx_k_norm, sinks, w_a, w_b, w_c, w_o, mlp_norm, w_up, w_down, final_norm):
    B, L, D = x.shape
    depth = w_in.shape[0]
    assert D == D_MODEL and L % (16 * BLOCK) == 0
    tabs = _rope_tables(positions)
    bf = jnp.bfloat16
    h = x
    for l in range(depth):
        wt, wT, wg, gk, wc = _layer_weights(w_in[l], idx_k_norm[l], w_c[l])
        (ka, ik, bq0, bq1, bq2, bk0, bk1, bk2, bv0, bv1, bv2, cq, ck, cv, qa, iq, va, iw) = _inproj(
            h, attn_norm[l][None, :], wt, wT, tabs, gk)
        oa = _dsa(ik, ka, va, iq, qa, iw)
        obs, lses = [], []
        for q_, k_, v_ in ((bq0, bk0, bv0), (bq1, bk1, bv1), (bq2, bk2, bv2)):
            o, lse = _dilated_group(q_, k_, v_)
            obs.append(o)
            lses.append(lse)
        oc = _sinkwin(cq, ck, cv, sinks[l])
        h2 = _mixer_out(h, attn_norm[l][None, :], wg, oa, obs, lses, oc,
                        w_a[l].astype(bf), w_b[l].astype(bf), wc, w_o[l].astype(bf))
        h3 = _ffn(h2.reshape(B * L, D), mlp_norm[l][None, :], w_up[l].astype(bf), w_down[l].astype(bf),
                  final_norm[None, :], l == depth - 1)
        h = h3.reshape(B, L, D)
    return h
```

```python
import functools

import jax
import jax.numpy as jnp
from jax import lax
from jax.experimental import pallas as pl
from jax.experimental.pallas import tpu as pltpu

D_MODEL = 1024
HEAD_DIM = 64
HALF = HEAD_DIM // 2
ROPE_THETA = 10000.0
NORM_EPS = 1e-6
A_HEADS = 6
IDX_HEADS = 8
IDX_DIM = 64
TOPK_MAX = 256
B_GROUPS = ((128, 1), (512, 4), (2048, 16))
B_HPG = 4
B_HEADS = B_HPG * len(B_GROUPS)
B_GW = B_HPG * HEAD_DIM
B_KEYS = 128
C_HEADS = 8
C_KV_HEADS = 2
C_GRP = C_HEADS // C_KV_HEADS
C_WINDOW = 128
N_BRANCHES = 3
D_FF = 4 * D_MODEL

BLOCK = 128
LANES = 128
INT_MIN = -2 ** 31
I16_MIN = -2 ** 15
VMEM_LIMIT = 56 * 1024 * 1024

_W_AQ = A_HEADS * HEAD_DIM
_W_IQ = IDX_HEADS * IDX_DIM
_W_B = B_HEADS * HEAD_DIM
_W_CQ = C_HEADS * HEAD_DIM
_W_CKV = C_KV_HEADS * HEAD_DIM
_OFF = {}
_o = 0
for _name, _w in (("aq", _W_AQ), ("ak", HEAD_DIM), ("av", HEAD_DIM), ("iq", _W_IQ), ("ik", IDX_DIM),
                  ("iw", IDX_HEADS), ("bq", _W_B), ("bk", _W_B), ("bv", _W_B), ("cq", _W_CQ),
                  ("ck", _W_CKV), ("cv", _W_CKV), ("gates", N_BRANCHES * D_MODEL)):
    _OFF[_name] = (_o, _o + _w)
    _o += _w

KC = 256
DEN_ROWS = 16
LOG2E = 1.4426950408889634
TM = 512


def _rmsnorm_f32(x, g):
    return x * lax.rsqrt(jnp.mean(x * x, axis=-1, keepdims=True) + NORM_EPS) * g


def _cparams(n_axes):
    return pltpu.CompilerParams(dimension_semantics=("arbitrary",) * n_axes, vmem_limit_bytes=VMEM_LIMIT)


def _inproj_kernel(x_ref, g_ref, wt_ref, wT_ref, cos_ref, sin_ref, cosT_ref, sinT_ref, gk_ref,
                   ka_ref, ik_ref,
                   bq0_ref, bq1_ref, bq2_ref, bk0_ref, bk1_ref, bk2_ref, bv0_ref, bv1_ref, bv2_ref,
                   cq_ref, ck_ref, cv_ref, qa_ref, iq_ref, va_ref, iw_ref, stage_ref):
    tm = x_ref.shape[1]
    u = _rmsnorm_f32(x_ref[0], g_ref[...]).astype(jnp.bfloat16)
    cos = cos_ref[0]
    sin = sin_ref[0]
    lane = lax.broadcasted_iota(jnp.int32, (1, LANES), 1)
    first_half = (lane % HEAD_DIM) < HALF

    def rope_tile(z):
        rot = jnp.where(first_half, pltpu.roll(z, LANES - HALF, 1), pltpu.roll(z, HALF, 1))
        return z * cos + rot * sin

    def project(c0, width):
        return jnp.dot(u, wt_ref[:, c0:c0 + width], preferred_element_type=jnp.float32)

    c0 = 0
    z = project(c0, 2 * HEAD_DIM)
    is_ik = lane >= HEAD_DIM
    ms = jnp.sum(jnp.where(is_ik, z * z, 0.0), axis=-1, keepdims=True) * (1.0 / IDX_DIM)
    z = z * jnp.where(is_ik, lax.rsqrt(ms + NORM_EPS) * gk_ref[...], 1.0)
    z = rope_tile(z)
    ka_ref[0] = z[:, :HEAD_DIM].astype(ka_ref.dtype)
    ik_ref[0] = z[:, HEAD_DIM:].astype(ik_ref.dtype)
    c0 += 2 * HEAD_DIM

    for refs, roped in (((bq0_ref, bq1_ref, bq2_ref), True), ((bk0_ref, bk1_ref, bk2_ref), True),
                        ((bv0_ref, bv1_ref, bv2_ref), False)):
        for g, (ref, (_, dil)) in enumerate(zip(refs, B_GROUPS)):
            z = project(c0 + g * B_GW, B_GW)
            if roped:
                z = jnp.concatenate([rope_tile(z[:, t * LANES:(t + 1) * LANES]) for t in range(B_GW // LANES)], axis=1)
            if dil == 1:
                ref[0, 0] = z.astype(ref.dtype)
            else:
                for t in range(B_GW // LANES):
                    stage_ref[t] = z[:, t * LANES:(t + 1) * LANES]
                for r in range(dil):
                    for t in range(B_GW // LANES):
                        ref[0, r, :, t * LANES:(t + 1) * LANES] = (
                            stage_ref[t, pl.ds(r, tm // dil, stride=dil), :].astype(ref.dtype))
        c0 += _W_B

    for ref, width, roped in ((cq_ref, _W_CQ, True), (ck_ref, _W_CKV, True), (cv_ref, _W_CKV, False)):
        z = project(c0, width)
        if roped:
            for t in range(width // LANES):
                ref[0, :, t * LANES:(t + 1) * LANES] = rope_tile(z[:, t * LANES:(t + 1) * LANES]).astype(ref.dtype)
        else:
            ref[0] = z.astype(ref.dtype)
        c0 += width

    def project_t(r0, rows):
        return lax.dot_general(wT_ref[r0:r0 + rows, :], u, (((1,), (1,)), ((), ())),
                               preferred_element_type=jnp.float32)

    cosT = cosT_ref[0]
    sinT = sinT_ref[0]

    def rope_rows(zt, ref, heads):
        for h in range(heads):
            x1 = zt[h * HEAD_DIM:h * HEAD_DIM + HALF]
            x2 = zt[h * HEAD_DIM + HALF:(h + 1) * HEAD_DIM]
            o1 = (x1 * cosT - x2 * sinT).astype(ref.dtype)
            o2 = (x2 * cosT + x1 * sinT).astype(ref.dtype)
            for c in range(tm // KC):
                ref[0, c, 0:HALF, h * KC:(h + 1) * KC] = o1[:, c * KC:(c + 1) * KC]
                ref[0, c, HALF:HEAD_DIM, h * KC:(h + 1) * KC] = o2[:, c * KC:(c + 1) * KC]

    zt = project_t(0, wT_ref.shape[0])
    r0 = 0
    rope_rows(zt[r0:r0 + _W_AQ], qa_ref, A_HEADS)
    r0 += _W_AQ
    rope_rows(zt[r0:r0 + _W_IQ], iq_ref, IDX_HEADS)
    r0 += _W_IQ
    zv = zt[r0:r0 + HEAD_DIM]
    for c in range(tm // KC):
        va_ref[0, c] = zv[:, c * KC:(c + 1) * KC].astype(va_ref.dtype)
    r0 += HEAD_DIM
    zw = zt[r0:r0 + IDX_HEADS] * (IDX_HEADS ** -0.5)
    for c in range(tm // KC):
        for h in range(IDX_HEADS):
            iw_ref[0, c, :, h * KC:(h + 1) * KC] = zw[h:h + 1, c * KC:(c + 1) * KC]


def _inproj(h, g, wt, wT, tabs, gk):
    B, L, D = h.shape
    tm = TM
    nq = L // KC
    bf = jnp.bfloat16
    tok = lambda w: pl.BlockSpec((1, tm, w), lambda b, i: (b, i, 0))
    tr = lambda r: pl.BlockSpec((1, r, tm), lambda b, i: (b, 0, i))
    full2 = lambda a: pl.BlockSpec(a.shape, lambda b, i: (0, 0))
    qtile = lambda rows, w: pl.BlockSpec((1, tm // KC, rows, w), lambda b, i: (b, i, 0, 0))
    grp_shape = lambda dil: jax.ShapeDtypeStruct((B, dil, L // dil, B_GW), bf)
    grp_spec = lambda dil: pl.BlockSpec((1, dil, tm // dil, B_GW), lambda b, i: (b, 0, i, 0))
    dils = [d for _, d in B_GROUPS]
    out_shape = (
        [jax.ShapeDtypeStruct((B, L, HEAD_DIM), bf), jax.ShapeDtypeStruct((B, L, IDX_DIM), bf)]
        + [grp_shape(d) for d in dils] * 3
        + [jax.ShapeDtypeStruct((B, L, _W_CQ), bf), jax.ShapeDtypeStruct((B, L, _W_CKV), bf),
           jax.ShapeDtypeStruct((B, L, _W_CKV), bf),
           jax.ShapeDtypeStruct((B, nq, HEAD_DIM, A_HEADS * KC), bf),
           jax.ShapeDtypeStruct((B, nq, IDX_DIM, IDX_HEADS * KC), bf),
           jax.ShapeDtypeStruct((B, nq, HEAD_DIM, KC), bf),
           jax.ShapeDtypeStruct((B, nq, 1, IDX_HEADS * KC), jnp.float32)])
    out_specs = (
        [tok(HEAD_DIM), tok(IDX_DIM)] + [grp_spec(d) for d in dils] * 3
        + [tok(_W_CQ), tok(_W_CKV), tok(_W_CKV),
           qtile(HEAD_DIM, A_HEADS * KC), qtile(IDX_DIM, IDX_HEADS * KC), qtile(HEAD_DIM, KC),
           qtile(1, IDX_HEADS * KC)])
    cos, sin, cosT, sinT = tabs
    return pl.pallas_call(
        _inproj_kernel,
        grid=(B, L // tm),
        in_specs=[tok(D), full2(g), full2(wt), full2(wT), tok(LANES), tok(LANES), tr(HALF), tr(HALF), full2(gk)],
        out_specs=out_specs,
        out_shape=out_shape,
        scratch_shapes=[pltpu.VMEM((B_GW // LANES, tm, LANES), jnp.float32)],
        compiler_params=_cparams(2),
        name="inproj",
    )(h, g, wt, wT, cos, sin, cosT, sinT, gk)


def _dsa_kernel(ik_ref, ka_ref, va_ref, iq_ref, qa_ref, iw_ref, o_ref,
                key_ref, hi_ref, lo_ref, r_ref, lg_ref, cm_ref, p_ref, m_ref, alpha_ref, acc_ref, outT_ref,
                *, topk):
    tq = o_ref.shape[1]
    qi = pl.program_id(1)
    nch = qi + 1
    npair = (nch + 1) // 2
    last = nch - 1
    col = qi * tq + lax.broadcasted_iota(jnp.int32, (KC, tq), 1)
    row0 = lax.broadcasted_iota(jnp.int32, (KC, tq), 0)
    f32 = jnp.float32

    def score_dot(j):
        s0 = pl.multiple_of(j * KC, KC)
        return jnp.dot(ik_ref[0, pl.ds(s0, KC), :], iq_ref[0, 0], preferred_element_type=f32)

    r_ref[0] = score_dot(0)

    def score_chunk(j, slot):
        s0 = pl.multiple_of(j * KC, KC)
        r_ref[1 - slot] = score_dot(jnp.minimum(j + 1, last))
        wr = jnp.maximum(r_ref[slot], 0.0) * iw_ref[0, 0]
        acc = wr[:, 0:tq]
        for h in range(1, IDX_HEADS):
            acc = acc + wr[:, h * tq:(h + 1) * tq]
        bits = lax.bitcast_convert_type(acc, jnp.int32)
        key = jnp.where(bits < 0, INT_MIN - bits, bits)
        key = jnp.where(row0 + s0 <= col, key, INT_MIN)
        key_ref[pl.ds(s0, KC), :] = key
        hi_ref[pl.ds(s0, KC), :] = lax.shift_right_arithmetic(key, 16).astype(jnp.int16)

    def score_pair(j2, carry):
        score_chunk(2 * j2, 0)
        score_chunk(jnp.minimum(2 * j2 + 1, last), 1)
        return carry

    lax.fori_loop(0, npair, score_pair, 0)

    @pl.when(nch % 2 == 1)
    def _():
        pad0 = pl.multiple_of(nch * KC, KC)
        hi_ref[pl.ds(pad0, KC), :] = jnp.full((KC, tq), I16_MIN, jnp.int16)
        lo_ref[pl.ds(pad0, KC), :] = jnp.full((KC, tq), I16_MIN, jnp.int16)

    one_b = jnp.ones((), jnp.bfloat16)
    zero_b = jnp.zeros((), jnp.bfloat16)
    rows = 2 * KC
    nacc = 4

    def count16(ref, cand16):
        def cnt_body(j, c):
            s0 = pl.multiple_of(j * rows, rows)
            w = jnp.where(ref[pl.ds(s0, rows), :] >= cand16, one_b, zero_b)
            accs = [w[a * 16:(a + 1) * 16] for a in range(nacc)]
            for i in range(nacc, rows // 16):
                accs[i % nacc] = accs[i % nacc] + w[i * 16:(i + 1) * 16]
            return c + ((accs[0] + accs[1]) + (accs[2] + accs[3])).astype(f32)
        c16 = lax.fori_loop(0, npair, cnt_body, jnp.zeros((16, tq), f32))
        return jnp.sum(c16, axis=0, keepdims=True)

    def bit_step(ref, k_want, shift, st):
        t_u, cnt, c_rej = st
        cand_u = t_u | lax.shift_left(jnp.int32(1), shift)
        c = count16(ref, (cand_u - 32768).astype(jnp.int16))
        take = c >= k_want
        return jnp.where(take, cand_u, t_u), jnp.where(take, c, cnt), jnp.where(take, c_rej, c)

    def search_bits(ref, k_want, cnt0):
        st0 = (jnp.zeros((1, tq), jnp.int32), cnt0, jnp.zeros((1, tq), f32))
        return lax.fori_loop(0, 16, lambda i, st: bit_step(ref, k_want, 15 - i, st), st0)

    def search_bracket(ref, k_want, cnt0, active):
        def finished(a, c_a, b):
            return jnp.logical_or(c_a == k_want, b - a <= 1)

        def cond(st):
            it, (a, c_a, b, _) = st
            pending = jnp.logical_and(active, jnp.logical_not(finished(a, c_a, b)))
            return jnp.logical_and(it < 16, jnp.max(pending.astype(f32)) > 0.0)

        def narrow(st, interpolate):
            a, c_a, b, c_b = st
            width = b - a
            if interpolate:
                frac = jnp.where(active, (c_a - k_want + 0.5) / (c_a - c_b), 0.5)
                g = a + (width.astype(f32) * frac).astype(jnp.int32)
            else:
                g = a + lax.shift_right_logical(width, 1)
            g = jnp.minimum(jnp.maximum(g, a + 1), jnp.maximum(b - 1, a + 1))
            c = count16(ref, (jnp.minimum(g, 65535) - 32768).astype(jnp.int16))
            live = jnp.logical_not(finished(a, c_a, b))
            up = jnp.logical_and(live, c >= k_want)
            down = jnp.logical_and(live, c < k_want)
            return jnp.where(up, g, a), jnp.where(up, c, c_a), jnp.where(down, g, b), jnp.where(down, c, c_b)

        def body(st):
            it, inner = st
            return it + 1, narrow(narrow(inner, True), False)

        st0 = (jnp.zeros((1, tq), jnp.int32), cnt0, jnp.full((1, tq), 65536, jnp.int32), jnp.zeros((1, tq), f32))
        a, c_a, _, c_b = lax.while_loop(cond, body, (jnp.int32(0), st0))[1]
        return a, c_a, c_b

    n_valid = (qi * tq + lax.broadcasted_iota(jnp.int32, (1, tq), 1) + 1).astype(f32)
    kf = jnp.full((1, tq), float(topk), f32)
    active = n_valid > kf
    th_u, cnt_h, gt_h = search_bits(hi_ref, kf, n_valid)
    th16 = (th_u - 32768).astype(jnp.int16)

    def lo_body(j, carry):
        s0 = pl.multiple_of(j * KC, KC)
        lo = ((key_ref[pl.ds(s0, KC), :] & 0xFFFF) - 32768).astype(jnp.int16)
        lo_ref[pl.ds(s0, KC), :] = jnp.where(hi_ref[pl.ds(s0, KC), :] == th16, lo, jnp.full_like(lo, I16_MIN))
        return carry

    lax.fori_loop(0, nch, lo_body, 0)
    active_lo = jnp.logical_and(active, cnt_h != kf)
    k_lo = jnp.where(active_lo, kf - gt_h, jnp.inf)
    tl_u, cnt_l, gt_l = search_bracket(lo_ref, k_lo, cnt_h - gt_h, active_lo)
    t32 = lax.shift_left(th_u - 32768, 16) | tl_u
    thr = jnp.maximum(t32, INT_MIN + 1)

    tied = jnp.logical_and(active_lo, cnt_l > k_lo)

    @pl.when(jnp.max(tied.astype(f32)) > 0.0)
    def _():
        need = k_lo - gt_l
        tri = (lax.broadcasted_iota(jnp.int32, (KC, KC), 1)
               < lax.broadcasted_iota(jnp.int32, (KC, KC), 0)).astype(jnp.bfloat16)

        def tie_body(j, run):
            s0 = pl.multiple_of(j * KC, KC)
            k = key_ref[pl.ds(s0, KC), :]
            eq = k == thr
            before = run + jnp.dot(tri, eq.astype(jnp.bfloat16), preferred_element_type=f32)
            drop = jnp.logical_and(jnp.logical_and(eq, tied), before >= need)
            key_ref[pl.ds(s0, KC), :] = jnp.where(drop, INT_MIN, k)
            return run + jnp.sum(eq.astype(f32), axis=0, keepdims=True)

        lax.fori_loop(0, nch, tie_body, jnp.zeros((1, tq), f32))

    def logits_stage(j, slot, valid):
        s0 = pl.multiple_of(j * KC, KC)
        lg = jnp.dot(ka_ref[0, pl.ds(s0, KC), :], qa_ref[0, 0], preferred_element_type=f32)
        sel = jnp.logical_and(key_ref[pl.ds(s0, KC), :] >= thr, valid)
        bias = jnp.where(sel, 0.0, -jnp.inf)
        for h in range(A_HEADS):
            lgh = lg[:, h * tq:(h + 1) * tq] + bias
            lg_ref[slot, :, h * tq:(h + 1) * tq] = lgh
            cm_ref[slot, :, h * tq:(h + 1) * tq] = jnp.max(lgh, axis=0, keepdims=True)

    m_ref[...] = jnp.full(m_ref.shape, -1e30, f32)
    alpha_ref[...] = jnp.ones(alpha_ref.shape, f32)
    acc_ref[...] = jnp.zeros(acc_ref.shape, f32)
    p_ref[1] = jnp.zeros(p_ref.shape[1:], p_ref.dtype)
    logits_stage(0, 0, True)
    ones_rows = (lax.broadcasted_iota(jnp.int32, (DEN_ROWS, KC), 0) == 0).astype(jnp.bfloat16)

    def value_product(j, slot):
        v_ext = jnp.concatenate([va_ref[0, j], ones_rows], axis=0)
        return jnp.dot(v_ext, p_ref[slot], preferred_element_type=f32)

    def att_chunk(jnext, valid_next, jprev, slot):
        logits_stage(jnext, 1 - slot, valid_next)
        acc_ref[...] = alpha_ref[...] * acc_ref[...] + value_product(jprev, 1 - slot)
        m_old = m_ref[...]
        m_new = jnp.maximum(m_old, cm_ref[slot])
        for h in range(A_HEADS):
            ph = jnp.exp2(lg_ref[slot, :, h * tq:(h + 1) * tq] - m_new[:, h * tq:(h + 1) * tq])
            p_ref[slot, :, h * tq:(h + 1) * tq] = ph.astype(p_ref.dtype)
        alpha_ref[...] = jnp.exp2(m_old - m_new)
        m_ref[...] = m_new

    def att_pair(j2, carry):
        ja = 2 * j2
        jb = jnp.minimum(ja + 1, last)
        att_chunk(jb, ja + 1 <= last, jnp.maximum(ja - 1, 0), 0)
        att_chunk(jnp.minimum(ja + 2, last), True, ja, 1)
        return carry

    lax.fori_loop(0, npair, att_pair, 0)
    acc = alpha_ref[...] * acc_ref[...] + value_product(last, 1)
    out = acc[0:HEAD_DIM] * (1.0 / acc[HEAD_DIM:HEAD_DIM + 1])
    for h in range(A_HEADS):
        outT_ref[h * HEAD_DIM:(h + 1) * HEAD_DIM, :] = out[:, h * tq:(h + 1) * tq]
    o_ref[0] = outT_ref[...].T.astype(o_ref.dtype)


def _dsa(ik, ka, va, iq, qa, iw):
    B, L, _ = ik.shape
    tq = KC
    topk = min(TOPK_MAX, L // 4)
    seq = lambda w: pl.BlockSpec((1, L, w), lambda b, i: (b, 0, 0))
    qtile = lambda rows, w: pl.BlockSpec((1, 1, rows, w), lambda b, i: (b, i, 0, 0))
    f32 = jnp.float32
    return pl.pallas_call(
        functools.partial(_dsa_kernel, topk=topk),
        grid=(B, L // tq),
        in_specs=[seq(IDX_DIM), seq(HEAD_DIM),
                  pl.BlockSpec((1, L // KC, HEAD_DIM, KC), lambda b, i: (b, 0, 0, 0)),
                  qtile(IDX_DIM, IDX_HEADS * tq), qtile(HEAD_DIM, A_HEADS * tq), qtile(1, IDX_HEADS * tq)],
        out_specs=pl.BlockSpec((1, tq, _W_AQ), lambda b, i: (b, i, 0)),
        out_shape=jax.ShapeDtypeStruct((B, L, _W_AQ), jnp.bfloat16),
        scratch_shapes=[pltpu.VMEM((L, tq), jnp.int32),
                        pltpu.VMEM((L + KC, tq), jnp.int16),
                        pltpu.VMEM((L + KC, tq), jnp.int16),
                        pltpu.VMEM((2, KC, IDX_HEADS * tq), f32),
                        pltpu.VMEM((2, KC, A_HEADS * tq), f32),
                        pltpu.VMEM((2, 1, A_HEADS * tq), f32),
                        pltpu.VMEM((2, KC, A_HEADS * tq), jnp.bfloat16),
                        pltpu.VMEM((1, A_HEADS * tq), f32),
                        pltpu.VMEM((1, A_HEADS * tq), f32),
                        pltpu.VMEM((HEAD_DIM + DEN_ROWS, A_HEADS * tq), f32),
                        pltpu.VMEM((_W_AQ, tq), f32)],
        compiler_params=_cparams(2),
        name="dsa",
    )(ik, ka, va, iq, qa, iw)


def _dilated_kernel(q_ref, kc_ref, kp_ref, vc_ref, vp_ref, o_ref, lse_ref):
    tqb = q_ref.shape[2]
    w = q_ref.shape[3]
    ib = pl.program_id(2)
    head_of_lane = lax.broadcasted_iota(jnp.int32, (1, w), 1) // HEAD_DIM
    nrow = B_HPG * BLOCK
    rr = lax.broadcasted_iota(jnp.int32, (nrow, 2 * BLOCK), 0) % BLOCK
    cc = lax.broadcasted_iota(jnp.int32, (nrow, 2 * BLOCK), 1)
    band = (cc >= rr) & (cc <= rr + B_KEYS)
    for r in range(q_ref.shape[1]):
        q = q_ref[0, r]
        kcat = jnp.concatenate([kp_ref[0, r], kc_ref[0, r]], axis=0)
        vcat = jnp.concatenate([vp_ref[0, r], vc_ref[0, r]], axis=0)
        for sb in range(tqb // BLOCK):
            qs = q[sb * BLOCK:(sb + 1) * BLOCK]
            ks = kcat[sb * BLOCK:sb * BLOCK + 2 * BLOCK]
            vs = vcat[sb * BLOCK:sb * BLOCK + 2 * BLOCK]
            mask = band
            if sb == 0:
                mask = band & ((cc >= BLOCK) | (ib > 0))
            q4 = jnp.concatenate([jnp.where(head_of_lane == hh, qs, jnp.zeros_like(qs)) for hh in range(B_HPG)],
                                 axis=0)
            lg = lax.dot_general(q4, ks, (((1,), (1,)), ((), ())), preferred_element_type=jnp.float32)
            lg = jnp.where(mask, lg, -jnp.inf)
            m = jnp.max(lg, axis=-1, keepdims=True)
            e = jnp.exp(lg - m)
            den = jnp.sum(e, axis=-1, keepdims=True)
            pv = jnp.dot(e.astype(vs.dtype), vs, preferred_element_type=jnp.float32) * (1.0 / den)
            lse = m + jnp.log(den)
            o_acc = pv[0:BLOCK]
            lse_acc = jnp.broadcast_to(lse[0:BLOCK], (BLOCK, w))
            for hh in range(1, B_HPG):
                hm = head_of_lane == hh
                o_acc = jnp.where(hm, pv[hh * BLOCK:(hh + 1) * BLOCK], o_acc)
                lse_acc = jnp.where(hm, lse[hh * BLOCK:(hh + 1) * BLOCK], lse_acc)
            o_ref[0, r, sb * BLOCK:(sb + 1) * BLOCK, :] = o_acc
            lse_ref[0, r, sb * BLOCK:(sb + 1) * BLOCK, :] = lse_acc


def _dilated_group(bq, bk, bv):
    B, dil, ld, w = bq.shape
    tqb = min(512, ld)
    per = tqb // BLOCK
    rps = min(dil, 512 // tqb)
    cur = pl.BlockSpec((1, rps, tqb, w), lambda b, r, i: (b, r, i, 0))
    prev = pl.BlockSpec((1, rps, BLOCK, w), lambda b, r, i: (b, r, jnp.maximum(i * per - 1, 0), 0))
    return pl.pallas_call(
        _dilated_kernel,
        grid=(B, dil // rps, ld // tqb),
        in_specs=[cur, cur, prev, cur, prev],
        out_specs=(cur, cur),
        out_shape=(jax.ShapeDtypeStruct((B, dil, ld, w), jnp.float32),) * 2,
        compiler_params=_cparams(3),
        name=f"dilated{dil}",
    )(bq, bk, bk, bv, bv)


def _sinkwin_kernel(sink_ref, q_ref, kc_ref, kp_ref, vc_ref, vp_ref, o_ref):
    tqb = q_ref.shape[1]
    ib = pl.program_id(1)
    q = q_ref[0]
    kcat = jnp.concatenate([kp_ref[0], kc_ref[0]], axis=0)
    vcat = jnp.concatenate([vp_ref[0], vc_ref[0]], axis=0)
    kv_of_lane = lax.broadcasted_iota(jnp.int32, (1, LANES), 1) // HEAD_DIM
    rr = lax.broadcasted_iota(jnp.int32, (BLOCK, 2 * BLOCK), 0)
    cc = lax.broadcasted_iota(jnp.int32, (BLOCK, 2 * BLOCK), 1)
    band = (cc > rr) & (cc <= rr + C_WINDOW)
    for sb in range(tqb // BLOCK):
        ks = kcat[sb * BLOCK:sb * BLOCK + 2 * BLOCK]
        vs = vcat[sb * BLOCK:sb * BLOCK + 2 * BLOCK]
        mask = band
        if sb == 0:
            mask = band & ((cc >= BLOCK) | (ib > 0))
        for g in range(C_GRP):
            qs = q[sb * BLOCK:(sb + 1) * BLOCK, g * LANES:(g + 1) * LANES]
            o_acc = jnp.zeros((BLOCK, LANES), jnp.float32)
            for kk in range(C_KV_HEADS):
                hm = kv_of_lane == kk
                sink = sink_ref[kk * C_GRP + g]
                qm = jnp.where(hm, qs, jnp.zeros_like(qs))
                lg = lax.dot_general(qm, ks, (((1,), (1,)), ((), ())), preferred_element_type=jnp.float32)
                lg = jnp.where(mask, lg, -jnp.inf)
                m = jnp.maximum(jnp.max(lg, axis=-1, keepdims=True), sink)
                e = jnp.exp(lg - m)
                den = jnp.sum(e, axis=-1, keepdims=True) + jnp.exp(sink - m)
                pv = jnp.dot(e.astype(vs.dtype), vs, preferred_element_type=jnp.float32)
                o_acc = jnp.where(hm, pv * (1.0 / den), o_acc)
            o_ref[0, sb * BLOCK:(sb + 1) * BLOCK, g * LANES:(g + 1) * LANES] = o_acc.astype(o_ref.dtype)


def _sinkwin(cq, ck, cv, sinks):
    B, L, _ = cq.shape
    tqb = min(512, L)
    per = tqb // BLOCK
    cur = lambda w: pl.BlockSpec((1, tqb, w), lambda b, i: (b, i, 0))
    prev = lambda w: pl.BlockSpec((1, BLOCK, w), lambda b, i: (b, jnp.maximum(i * per - 1, 0), 0))
    return pl.pallas_call(
        _sinkwin_kernel,
        grid=(B, L // tqb),
        in_specs=[pl.BlockSpec(memory_space=pltpu.SMEM),
                  cur(_W_CQ), cur(_W_CKV), prev(_W_CKV), cur(_W_CKV), prev(_W_CKV)],
        out_specs=cur(_W_CQ),
        out_shape=jax.ShapeDtypeStruct((B, L, _W_CQ), jnp.bfloat16),
        compiler_params=_cparams(2),
        name="sinkwin",
    )(sinks, cq, ck, ck, cv, cv)


def _mixer_out_kernel(h_ref, g_ref, wg_ref, oa_ref, ob0_ref, ob1_ref, ob2_ref, l0_ref, l1_ref, l2_ref,
                      oc_ref, wa_ref, wb_ref, wc_ref, wo_ref, out_ref, *stage_refs):
    tm = h_ref.shape[1]
    h = h_ref[0]
    u = _rmsnorm_f32(h, g_ref[...]).astype(jnp.bfloat16)

    def token_major(ref, stage_ref):
        dil = ref.shape[1]
        for r in range(dil):
            for t in range(B_GW // LANES):
                stage_ref[t, pl.ds(r, tm // dil, stride=dil), :] = ref[0, r, :, t * LANES:(t + 1) * LANES]
        return jnp.concatenate([stage_ref[t] for t in range(B_GW // LANES)], axis=1)

    ob0, l0 = ob0_ref[0, 0], l0_ref[0, 0]
    ob1, l1 = token_major(ob1_ref, stage_refs[0]), token_major(l1_ref, stage_refs[1])
    ob2, l2 = token_major(ob2_ref, stage_refs[2]), token_major(l2_ref, stage_refs[3])
    mx = jnp.maximum(jnp.maximum(l0, l1), l2)
    e0, e1, e2 = jnp.exp(l0 - mx), jnp.exp(l1 - mx), jnp.exp(l2 - mx)
    ob = (e0 * ob0 + e1 * ob1 + e2 * ob2) * (1.0 / (e0 + e1 + e2))

    def gate(i):
        z = jnp.dot(u, wg_ref[:, i * D_MODEL:(i + 1) * D_MODEL], preferred_element_type=jnp.float32)
        return jax.nn.sigmoid(z)

    merged = gate(0) * jnp.dot(oa_ref[0], wa_ref[...], preferred_element_type=jnp.float32)
    merged = merged + gate(1) * jnp.dot(ob.astype(jnp.bfloat16), wb_ref[...], preferred_element_type=jnp.float32)
    merged = merged + gate(2) * jnp.dot(oc_ref[0], wc_ref[...], preferred_element_type=jnp.float32)
    out_ref[0] = h + jnp.dot(merged.astype(jnp.bfloat16), wo_ref[...], preferred_element_type=jnp.float32)


def _mixer_out(h, g, wg, oa, obs, lses, oc, wa, wb, wc, wo):
    B, L, D = h.shape
    tm = TM
    tok = lambda w: pl.BlockSpec((1, tm, w), lambda b, i: (b, i, 0))
    full = lambda a: pl.BlockSpec(a.shape, lambda b, i: (0, 0))
    grp = lambda a: pl.BlockSpec((1, a.shape[1], tm // a.shape[1], B_GW), lambda b, i: (b, 0, i, 0))
    return pl.pallas_call(
        _mixer_out_kernel,
        grid=(B, L // tm),
        in_specs=[tok(D), full(g), full(wg), tok(_W_AQ)] + [grp(a) for a in obs] + [grp(a) for a in lses]
                 + [tok(_W_CQ), full(wa), full(wb), full(wc), full(wo)],
        out_specs=tok(D),
        out_shape=jax.ShapeDtypeStruct((B, L, D), jnp.float32),
        scratch_shapes=[pltpu.VMEM((B_GW // LANES, tm, LANES), jnp.float32)] * 4,
        compiler_params=_cparams(2),
        name="mixer_out",
    )(h, g, wg, oa, *obs, *lses, oc, wa, wb, wc, wo)


def _ffn_kernel(h_ref, g_ref, wup_ref, wdn_ref, gf_ref, out_ref, *, final, chunk):
    h = h_ref[...]
    u = _rmsnorm_f32(h, g_ref[...]).astype(jnp.bfloat16)
    acc = h
    for c in range(D_FF // chunk):
        up = jnp.dot(u, wup_ref[:, c * chunk:(c + 1) * chunk], preferred_element_type=jnp.float32)
        act = jnp.square(jnp.maximum(up, 0.0)).astype(jnp.bfloat16)
        acc = acc + jnp.dot(act, wdn_ref[c * chunk:(c + 1) * chunk, :], preferred_element_type=jnp.float32)
    if final:
        acc = _rmsnorm_f32(acc, gf_ref[...])
    out_ref[...] = acc


def _ffn(h2d, g, wup, wdn, gf, final):
    N, D = h2d.shape
    tm = TM
    row = pl.BlockSpec((tm, D), lambda i: (i, 0))
    full = lambda a: pl.BlockSpec(a.shape, lambda i: (0, 0))
    return pl.pallas_call(
        functools.partial(_ffn_kernel, final=final, chunk=1024),
        grid=(N // tm,),
        in_specs=[row, full(g), full(wup), full(wdn), full(gf)],
        out_specs=row,
        out_shape=jax.ShapeDtypeStruct((N, D), jnp.float32),
        compiler_params=_cparams(1),
        name="ffn",
    )(h2d, g, wup, wdn, gf)


def _rope_tables(positions):
    inv = ROPE_THETA ** (-jnp.arange(HALF, dtype=jnp.float32) / HALF)
    ang = positions.astype(jnp.float32)[..., None] * inv
    c, s = jnp.cos(ang), jnp.sin(ang)
    reps = LANES // HEAD_DIM
    cos = jnp.tile(jnp.concatenate([c, c], axis=-1), (1, 1, reps))
    sin = jnp.tile(jnp.concatenate([-s, s], axis=-1), (1, 1, reps))
    return cos, sin, jnp.swapaxes(c, 1, 2), jnp.swapaxes(s, 1, 2)


def _layer_weights(w_in, idx_k_norm, w_c):
    col = lambda n: w_in[:, _OFF[n][0]:_OFF[n][1]]
    scale = HEAD_DIM ** -0.5
    idx_scale = IDX_DIM ** -0.5
    cq = col("cq").reshape(D_MODEL, C_KV_HEADS, C_GRP, HEAD_DIM).transpose(0, 2, 1, 3).reshape(D_MODEL, _W_CQ)
    wt = jnp.concatenate([col("ak"), col("ik"), col("bq") * scale, col("bk"), col("bv"),
                          cq * scale, col("ck"), col("cv")], axis=1).astype(jnp.bfloat16)
    wT = jnp.concatenate([col("aq") * (scale * LOG2E), col("iq") * idx_scale, col("av"), col("iw")],
                         axis=1).T.astype(jnp.bfloat16)
    wg = col("gates").astype(jnp.bfloat16)
    gk = jnp.concatenate([jnp.ones((IDX_DIM,), jnp.float32), idx_k_norm])[None, :]
    wc = w_c.reshape(C_KV_HEADS, C_GRP, HEAD_DIM, D_MODEL).transpose(1, 0, 2, 3).reshape(_W_CQ, D_MODEL)
    return wt, wT, wg, gk, wc.astype(jnp.bfloat16)


def kernel(x, positions, attn_norm, w_in, idx_k_norm, sinks, w_a, w_b, w_c, w_o, mlp_norm, w_up, w_down, final_norm):
    B, L, D = x.shape
    depth = w_in.shape[0]
    assert D == D_MODEL and L % (16 * BLOCK) == 0
    tabs = _rope_tables(positions)
    bf = jnp.bfloat16
    h = x
    for l in range(depth):
        wt, wT, wg, gk, wc = _layer_weights(w_in[l], idx_k_norm[l], w_c[l])
        (ka, ik, bq0, bq1, bq2, bk0, bk1, bk2, bv0, bv1, bv2, cq, ck, cv, qa, iq, va, iw) = _inproj(
            h, attn_norm[l][None, :], wt, wT, tabs, gk)
        oa = _dsa(ik, ka, va, iq, qa, iw)
        obs, lses = [], []
        for q_, k_, v_ in ((bq0, bk0, bv0), (bq1, bk1, bv1), (bq2, bk2, bv2)):
            o, lse = _dilated_group(q_, k_, v_)
            obs.append(o)
            lses.append(lse)
        oc = _sinkwin(cq, ck, cv, sinks[l])
        h2 = _mixer_out(h, attn_norm[l][None, :], wg, oa, obs, lses, oc,
                        w_a[l].astype(bf), w_b[l].astype(bf), wc, w_o[l].astype(bf))
        h3 = _ffn(h2.reshape(B * L, D), mlp_norm[l][None, :], w_up[l].astype(bf), w_down[l].astype(bf),
                  final_norm[None, :], l == depth - 1)
        h = h3.reshape(B, L, D)
    return h
```

```python
import functools

import jax
import jax.numpy as jnp
from jax import lax
from jax.experimental import pallas as pl
from jax.experimental.pallas import tpu as pltpu

D_MODEL = 1024
HEAD_DIM = 64
HALF = HEAD_DIM // 2
ROPE_THETA = 10000.0
NORM_EPS = 1e-6
A_HEADS = 6
IDX_HEADS = 8
IDX_DIM = 64
TOPK_MAX = 256
B_GROUPS = ((128, 1), (512, 4), (2048, 16))
B_HPG = 4
B_HEADS = B_HPG * len(B_GROUPS)
B_GW = B_HPG * HEAD_DIM
B_KEYS = 128
C_HEADS = 8
C_KV_HEADS = 2
C_GRP = C_HEADS // C_KV_HEADS
C_WINDOW = 128
N_BRANCHES = 3
D_FF = 4 * D_MODEL

BLOCK = 128
LANES = 128
INT_MIN = -2 ** 31
I16_MIN = -2 ** 15
VMEM_LIMIT = 56 * 1024 * 1024

_W_AQ = A_HEADS * HEAD_DIM
_W_IQ = IDX_HEADS * IDX_DIM
_W_B = B_HEADS * HEAD_DIM
_W_CQ = C_HEADS * HEAD_DIM
_W_CKV = C_KV_HEADS * HEAD_DIM
_OFF = {}
_o = 0
for _name, _w in (("aq", _W_AQ), ("ak", HEAD_DIM), ("av", HEAD_DIM), ("iq", _W_IQ), ("ik", IDX_DIM),
                  ("iw", IDX_HEADS), ("bq", _W_B), ("bk", _W_B), ("bv", _W_B), ("cq", _W_CQ),
                  ("ck", _W_CKV), ("cv", _W_CKV), ("gates", N_BRANCHES * D_MODEL)):
    _OFF[_name] = (_o, _o + _w)
    _o += _w

KC = 256
DEN_ROWS = 16
LOG2E = 1.4426950408889634
TM = 512


def _rmsnorm_f32(x, g):
    return x * lax.rsqrt(jnp.mean(x * x, axis=-1, keepdims=True) + NORM_EPS) * g


def _cparams(n_axes):
    return pltpu.CompilerParams(dimension_semantics=("arbitrary",) * n_axes, vmem_limit_bytes=VMEM_LIMIT)


def _inproj_kernel(x_ref, g_ref, wt_ref, wT_ref, cos_ref, sin_ref, cosT_ref, sinT_ref, gk_ref,
                   ka_ref, ik_ref,
                   bq0_ref, bq1_ref, bq2_ref, bk0_ref, bk1_ref, bk2_ref, bv0_ref, bv1_ref, bv2_ref,
                   cq_ref, ck_ref, cv_ref, qa_ref, iq_ref, va_ref, iw_ref, stage_ref):
    tm = x_ref.shape[1]
    u = _rmsnorm_f32(x_ref[0], g_ref[...]).astype(jnp.bfloat16)
    cos = cos_ref[0]
    sin = sin_ref[0]
    lane = lax.broadcasted_iota(jnp.int32, (1, LANES), 1)
    first_half = (lane % HEAD_DIM) < HALF

    def rope_tile(z):
        rot = jnp.where(first_half, pltpu.roll(z, LANES - HALF, 1), pltpu.roll(z, HALF, 1))
        return z * cos + rot * sin

    def project(c0, width):
        return jnp.dot(u, wt_ref[:, c0:c0 + width], preferred_element_type=jnp.float32)

    c0 = 0
    z = project(c0, 2 * HEAD_DIM)
    is_ik = lane >= HEAD_DIM
    ms = jnp.sum(jnp.where(is_ik, z * z, 0.0), axis=-1, keepdims=True) * (1.0 / IDX_DIM)
    z = z * jnp.where(is_ik, lax.rsqrt(ms + NORM_EPS) * gk_ref[...], 1.0)
    z = rope_tile(z)
    ka_ref[0] = z[:, :HEAD_DIM].astype(ka_ref.dtype)
    ik_ref[0] = z[:, HEAD_DIM:].astype(ik_ref.dtype)
    c0 += 2 * HEAD_DIM

    for refs, roped in (((bq0_ref, bq1_ref, bq2_ref), True), ((bk0_ref, bk1_ref, bk2_ref), True),
                        ((bv0_ref, bv1_ref, bv2_ref), False)):
        for g, (ref, (_, dil)) in enumerate(zip(refs, B_GROUPS)):
            z = project(c0 + g * B_GW, B_GW)
            if roped:
                z = jnp.concatenate([rope_tile(z[:, t * LANES:(t + 1) * LANES]) for t in range(B_GW // LANES)], axis=1)
            if dil == 1:
                ref[0, 0] = z.astype(ref.dtype)
            else:
                for t in range(B_GW // LANES):
                    stage_ref[t] = z[:, t * LANES:(t + 1) * LANES]
                for r in range(dil):
                    for t in range(B_GW // LANES):
                        ref[0, r, :, t * LANES:(t + 1) * LANES] = (
                            stage_ref[t, pl.ds(r, tm // dil, stride=dil), :].astype(ref.dtype))
        c0 += _W_B

    for ref, width, roped in ((cq_ref, _W_CQ, True), (ck_ref, _W_CKV, True), (cv_ref, _W_CKV, False)):
        z = project(c0, width)
        if roped:
            for t in range(width // LANES):
                ref[0, :, t * LANES:(t + 1) * LANES] = rope_tile(z[:, t * LANES:(t + 1) * LANES]).astype(ref.dtype)
        else:
            ref[0] = z.astype(ref.dtype)
        c0 += width

    def project_t(r0, rows):
        return lax.dot_general(wT_ref[r0:r0 + rows, :], u, (((1,), (1,)), ((), ())),
                               preferred_element_type=jnp.float32)

    cosT = cosT_ref[0]
    sinT = sinT_ref[0]

    def rope_rows(zt, ref, heads):
        for h in range(heads):
            x1 = zt[h * HEAD_DIM:h * HEAD_DIM + HALF]
            x2 = zt[h * HEAD_DIM + HALF:(h + 1) * HEAD_DIM]
            o1 = (x1 * cosT - x2 * sinT).astype(ref.dtype)
            o2 = (x2 * cosT + x1 * sinT).astype(ref.dtype)
            for c in range(tm // KC):
                ref[0, c, 0:HALF, h * KC:(h + 1) * KC] = o1[:, c * KC:(c + 1) * KC]
                ref[0, c, HALF:HEAD_DIM, h * KC:(h + 1) * KC] = o2[:, c * KC:(c + 1) * KC]

    zt = project_t(0, wT_ref.shape[0])
    r0 = 0
    rope_rows(zt[r0:r0 + _W_AQ], qa_ref, A_HEADS)
    r0 += _W_AQ
    rope_rows(zt[r0:r0 + _W_IQ], iq_ref, IDX_HEADS)
    r0 += _W_IQ
    zv = zt[r0:r0 + HEAD_DIM]
    for c in range(tm // KC):
        va_ref[0, c] = zv[:, c * KC:(c + 1) * KC].astype(va_ref.dtype)
    r0 += HEAD_DIM
    zw = zt[r0:r0 + IDX_HEADS] * (IDX_HEADS ** -0.5)
    for c in range(tm // KC):
        for h in range(IDX_HEADS):
            iw_ref[0, c, :, h * KC:(h + 1) * KC] = zw[h:h + 1, c * KC:(c + 1) * KC]


def _inproj(h, g, wt, wT, tabs, gk):
    B, L, D = h.shape
    tm = TM
    nq = L // KC
    bf = jnp.bfloat16
    tok = lambda w: pl.BlockSpec((1, tm, w), lambda b, i: (b, i, 0))
    tr = lambda r: pl.BlockSpec((1, r, tm), lambda b, i: (b, 0, i))
    full2 = lambda a: pl.BlockSpec(a.shape, lambda b, i: (0, 0))
    qtile = lambda rows, w: pl.BlockSpec((1, tm // KC, rows, w), lambda b, i: (b, i, 0, 0))
    grp_shape = lambda dil: jax.ShapeDtypeStruct((B, dil, L // dil, B_GW), bf)
    grp_spec = lambda dil: pl.BlockSpec((1, dil, tm // dil, B_GW), lambda b, i: (b, 0, i, 0))
    dils = [d for _, d in B_GROUPS]
    out_shape = (
        [jax.ShapeDtypeStruct((B, L, HEAD_DIM), bf), jax.ShapeDtypeStruct((B, L, IDX_DIM), bf)]
        + [grp_shape(d) for d in dils] * 3
        + [jax.ShapeDtypeStruct((B, L, _W_CQ), bf), jax.ShapeDtypeStruct((B, L, _W_CKV), bf),
           jax.ShapeDtypeStruct((B, L, _W_CKV), bf),
           jax.ShapeDtypeStruct((B, nq, HEAD_DIM, A_HEADS * KC), bf),
           jax.ShapeDtypeStruct((B, nq, IDX_DIM, IDX_HEADS * KC), bf),
           jax.ShapeDtypeStruct((B, nq, HEAD_DIM, KC), bf),
           jax.ShapeDtypeStruct((B, nq, 1, IDX_HEADS * KC), jnp.float32)])
    out_specs = (
        [tok(HEAD_DIM), tok(IDX_DIM)] + [grp_spec(d) for d in dils] * 3
        + [tok(_W_CQ), tok(_W_CKV), tok(_W_CKV),
           qtile(HEAD_DIM, A_HEADS * KC), qtile(IDX_DIM, IDX_HEADS * KC), qtile(HEAD_DIM, KC),
           qtile(1, IDX_HEADS * KC)])
    cos, sin, cosT, sinT = tabs
    return pl.pallas_call(
        _inproj_kernel,
        grid=(B, L // tm),
        in_specs=[tok(D), full2(g), full2(wt), full2(wT), tok(LANES), tok(LANES), tr(HALF), tr(HALF), full2(gk)],
        out_specs=out_specs,
        out_shape=out_shape,
        scratch_shapes=[pltpu.VMEM((B_GW // LANES, tm, LANES), jnp.float32)],
        compiler_params=_cparams(2),
        name="inproj",
    )(h, g, wt, wT, cos, sin, cosT, sinT, gk)


def _dsa_kernel(ik_ref, ka_ref, va_ref, iq_ref, qa_ref, iw_ref, o_ref,
                key_ref, hi_ref, lo_ref, r_ref, lg_ref, cm_ref, p_ref, m_ref, alpha_ref, acc_ref, outT_ref,
                *, topk):
    tq = o_ref.shape[1]
    qi = pl.program_id(1)
    nch = qi + 1
    npair = (nch + 1) // 2
    last = nch - 1
    col = qi * tq + lax.broadcasted_iota(jnp.int32, (KC, tq), 1)
    row0 = lax.broadcasted_iota(jnp.int32, (KC, tq), 0)
    f32 = jnp.float32

    def score_dot(j):
        s0 = pl.multiple_of(j * KC, KC)
        return jnp.dot(ik_ref[0, pl.ds(s0, KC), :], iq_ref[0, 0], preferred_element_type=f32)

    r_ref[0] = score_dot(0)

    def score_chunk(j, slot):
        s0 = pl.multiple_of(j * KC, KC)
        r_ref[1 - slot] = score_dot(jnp.minimum(j + 1, last))
        wr = jnp.maximum(r_ref[slot], 0.0) * iw_ref[0, 0]
        acc = wr[:, 0:tq]
        for h in range(1, IDX_HEADS):
            acc = acc + wr[:, h * tq:(h + 1) * tq]
        bits = lax.bitcast_convert_type(acc, jnp.int32)
        key = jnp.where(bits < 0, INT_MIN - bits, bits)
        key = jnp.where(row0 + s0 <= col, key, INT_MIN)
        key_ref[pl.ds(s0, KC), :] = key
        hi_ref[pl.ds(s0, KC), :] = lax.shift_right_arithmetic(key, 16).astype(jnp.int16)

    def score_pair(j2, carry):
        score_chunk(2 * j2, 0)
        score_chunk(jnp.minimum(2 * j2 + 1, last), 1)
        return carry

    lax.fori_loop(0, npair, score_pair, 0)

    @pl.when(nch % 2 == 1)
    def _():
        pad0 = pl.multiple_of(nch * KC, KC)
        hi_ref[pl.ds(pad0, KC), :] = jnp.full((KC, tq), I16_MIN, jnp.int16)
        lo_ref[pl.ds(pad0, KC), :] = jnp.full((KC, tq), I16_MIN, jnp.int16)

    one_b = jnp.ones((), jnp.bfloat16)
    zero_b = jnp.zeros((), jnp.bfloat16)
    rows = 2 * KC
    nacc = 4

    def count16(ref, cand16):
        def cnt_body(j, c):
            s0 = pl.multiple_of(j * rows, rows)
            w = jnp.where(ref[pl.ds(s0, rows), :] >= cand16, one_b, zero_b)
            accs = [w[a * 16:(a + 1) * 16] for a in range(nacc)]
            for i in range(nacc, rows // 16):
                accs[i % nacc] = accs[i % nacc] + w[i * 16:(i + 1) * 16]
            return c + ((accs[0] + accs[1]) + (accs[2] + accs[3])).astype(f32)
        c16 = lax.fori_loop(0, npair, cnt_body, jnp.zeros((16, tq), f32))
        return jnp.sum(c16, axis=0, keepdims=True)

    def bit_step(ref, k_want, shift, st):
        t_u, cnt, c_rej = st
        cand_u = t_u | lax.shift_left(jnp.int32(1), shift)
        c = count16(ref, (cand_u - 32768).astype(jnp.int16))
        take = c >= k_want
        return jnp.where(take, cand_u, t_u), jnp.where(take, c, cnt), jnp.where(take, c_rej, c)

    def search16(ref, k_want, cnt0, active, early_exit):
        st0 = (jnp.zeros((1, tq), jnp.int32), cnt0, jnp.zeros((1, tq), f32))
        if not early_exit:
            return lax.fori_loop(0, 16, lambda i, st: bit_step(ref, k_want, 15 - i, st), st0)

        def cond(st):
            g, (_, cnt, _) = st
            pending = jnp.logical_and(active, cnt != k_want)
            return jnp.logical_and(g < 4, jnp.max(pending.astype(f32)) > 0.0)

        def body(st):
            g, inner = st
            for k in range(4):
                inner = bit_step(ref, k_want, 15 - (4 * g + k), inner)
            return g + 1, inner

        return lax.while_loop(cond, body, (jnp.int32(0), st0))[1]

    n_valid = (qi * tq + lax.broadcasted_iota(jnp.int32, (1, tq), 1) + 1).astype(f32)
    kf = jnp.full((1, tq), float(topk), f32)
    active = n_valid > kf
    th_u, cnt_h, gt_h = search16(hi_ref, kf, n_valid, active, early_exit=False)
    th16 = (th_u - 32768).astype(jnp.int16)

    def lo_body(j, carry):
        s0 = pl.multiple_of(j * KC, KC)
        lo = ((key_ref[pl.ds(s0, KC), :] & 0xFFFF) - 32768).astype(jnp.int16)
        lo_ref[pl.ds(s0, KC), :] = jnp.where(hi_ref[pl.ds(s0, KC), :] == th16, lo, jnp.full_like(lo, I16_MIN))
        return carry

    lax.fori_loop(0, nch, lo_body, 0)
    active_lo = jnp.logical_and(active, cnt_h != kf)
    k_lo = jnp.where(active_lo, kf - gt_h, jnp.inf)
    tl_u, cnt_l, gt_l = search16(lo_ref, k_lo, cnt_h - gt_h, active_lo, early_exit=True)
    t32 = lax.shift_left(th_u - 32768, 16) | tl_u
    thr = jnp.maximum(t32, INT_MIN + 1)

    tied = jnp.logical_and(active_lo, cnt_l > k_lo)

    @pl.when(jnp.max(tied.astype(f32)) > 0.0)
    def _():
        need = k_lo - gt_l
        tri = (lax.broadcasted_iota(jnp.int32, (KC, KC), 1)
               < lax.broadcasted_iota(jnp.int32, (KC, KC), 0)).astype(jnp.bfloat16)

        def tie_body(j, run):
            s0 = pl.multiple_of(j * KC, KC)
            k = key_ref[pl.ds(s0, KC), :]
            eq = k == thr
            before = run + jnp.dot(tri, eq.astype(jnp.bfloat16), preferred_element_type=f32)
            drop = jnp.logical_and(jnp.logical_and(eq, tied), before >= need)
            key_ref[pl.ds(s0, KC), :] = jnp.where(drop, INT_MIN, k)
            return run + jnp.sum(eq.astype(f32), axis=0, keepdims=True)

        lax.fori_loop(0, nch, tie_body, jnp.zeros((1, tq), f32))

    def logits_stage(j, slot, valid):
        s0 = pl.multiple_of(j * KC, KC)
        lg = jnp.dot(ka_ref[0, pl.ds(s0, KC), :], qa_ref[0, 0], preferred_element_type=f32)
        sel = jnp.logical_and(key_ref[pl.ds(s0, KC), :] >= thr, valid)
        bias = jnp.where(sel, 0.0, -jnp.inf)
        for h in range(A_HEADS):
            lgh = lg[:, h * tq:(h + 1) * tq] + bias
            lg_ref[slot, :, h * tq:(h + 1) * tq] = lgh
            cm_ref[slot, :, h * tq:(h + 1) * tq] = jnp.max(lgh, axis=0, keepdims=True)

    m_ref[...] = jnp.full(m_ref.shape, -1e30, f32)
    alpha_ref[...] = jnp.ones(alpha_ref.shape, f32)
    acc_ref[...] = jnp.zeros(acc_ref.shape, f32)
    p_ref[1] = jnp.zeros(p_ref.shape[1:], p_ref.dtype)
    logits_stage(0, 0, True)
    ones_rows = (lax.broadcasted_iota(jnp.int32, (DEN_ROWS, KC), 0) == 0).astype(jnp.bfloat16)

    def value_product(j, slot):
        v_ext = jnp.concatenate([va_ref[0, j], ones_rows], axis=0)
        return jnp.dot(v_ext, p_ref[slot], preferred_element_type=f32)

    def att_chunk(jnext, valid_next, jprev, slot):
        logits_stage(jnext, 1 - slot, valid_next)
        acc_ref[...] = alpha_ref[...] * acc_ref[...] + value_product(jprev, 1 - slot)
        m_old = m_ref[...]
        m_new = jnp.maximum(m_old, cm_ref[slot])
        for h in range(A_HEADS):
            ph = jnp.exp2(lg_ref[slot, :, h * tq:(h + 1) * tq] - m_new[:, h * tq:(h + 1) * tq])
            p_ref[slot, :, h * tq:(h + 1) * tq] = ph.astype(p_ref.dtype)
        alpha_ref[...] = jnp.exp2(m_old - m_new)
        m_ref[...] = m_new

    def att_pair(j2, carry):
        ja = 2 * j2
        jb = jnp.minimum(ja + 1, last)
        att_chunk(jb, ja + 1 <= last, jnp.maximum(ja - 1, 0), 0)
        att_chunk(jnp.minimum(ja + 2, last), True, ja, 1)
        return carry

    lax.fori_loop(0, npair, att_pair, 0)
    acc = alpha_ref[...] * acc_ref[...] + value_product(last, 1)
    out = acc[0:HEAD_DIM] * (1.0 / acc[HEAD_DIM:HEAD_DIM + 1])
    for h in range(A_HEADS):
        outT_ref[h * HEAD_DIM:(h + 1) * HEAD_DIM, :] = out[:, h * tq:(h + 1) * tq]
    o_ref[0] = outT_ref[...].T.astype(o_ref.dtype)


def _dsa(ik, ka, va, iq, qa, iw):
    B, L, _ = ik.shape
    tq = KC
    topk = min(TOPK_MAX, L // 4)
    seq = lambda w: pl.BlockSpec((1, L, w), lambda b, i: (b, 0, 0))
    qtile = lambda rows, w: pl.BlockSpec((1, 1, rows, w), lambda b, i: (b, i, 0, 0))
    f32 = jnp.float32
    return pl.pallas_call(
        functools.partial(_dsa_kernel, topk=topk),
        grid=(B, L // tq),
        in_specs=[seq(IDX_DIM), seq(HEAD_DIM),
                  pl.BlockSpec((1, L // KC, HEAD_DIM, KC), lambda b, i: (b, 0, 0, 0)),
                  qtile(IDX_DIM, IDX_HEADS * tq), qtile(HEAD_DIM, A_HEADS * tq), qtile(1, IDX_HEADS * tq)],
        out_specs=pl.BlockSpec((1, tq, _W_AQ), lambda b, i: (b, i, 0)),
        out_shape=jax.ShapeDtypeStruct((B, L, _W_AQ), jnp.bfloat16),
        scratch_shapes=[pltpu.VMEM((L, tq), jnp.int32),
                        pltpu.VMEM((L + KC, tq), jnp.int16),
                        pltpu.VMEM((L + KC, tq), jnp.int16),
                        pltpu.VMEM((2, KC, IDX_HEADS * tq), f32),
                        pltpu.VMEM((2, KC, A_HEADS * tq), f32),
                        pltpu.VMEM((2, 1, A_HEADS * tq), f32),
                        pltpu.VMEM((2, KC, A_HEADS * tq), jnp.bfloat16),
                        pltpu.VMEM((1, A_HEADS * tq), f32),
                        pltpu.VMEM((1, A_HEADS * tq), f32),
                        pltpu.VMEM((HEAD_DIM + DEN_ROWS, A_HEADS * tq), f32),
                        pltpu.VMEM((_W_AQ, tq), f32)],
        compiler_params=_cparams(2),
        name="dsa",
    )(ik, ka, va, iq, qa, iw)


def _dilated_kernel(q_ref, kc_ref, kp_ref, vc_ref, vp_ref, o_ref, lse_ref):
    tqb = q_ref.shape[2]
    w = q_ref.shape[3]
    ib = pl.program_id(2)
    head_of_lane = lax.broadcasted_iota(jnp.int32, (1, w), 1) // HEAD_DIM
    nrow = B_HPG * BLOCK
    rr = lax.broadcasted_iota(jnp.int32, (nrow, 2 * BLOCK), 0) % BLOCK
    cc = lax.broadcasted_iota(jnp.int32, (nrow, 2 * BLOCK), 1)
    band = (cc >= rr) & (cc <= rr + B_KEYS)
    for r in range(q_ref.shape[1]):
        q = q_ref[0, r]
        kcat = jnp.concatenate([kp_ref[0, r], kc_ref[0, r]], axis=0)
        vcat = jnp.concatenate([vp_ref[0, r], vc_ref[0, r]], axis=0)
        for sb in range(tqb // BLOCK):
            qs = q[sb * BLOCK:(sb + 1) * BLOCK]
            ks = kcat[sb * BLOCK:sb * BLOCK + 2 * BLOCK]
            vs = vcat[sb * BLOCK:sb * BLOCK + 2 * BLOCK]
            mask = band
            if sb == 0:
                mask = band & ((cc >= BLOCK) | (ib > 0))
            q4 = jnp.concatenate([jnp.where(head_of_lane == hh, qs, jnp.zeros_like(qs)) for hh in range(B_HPG)],
                                 axis=0)
            lg = lax.dot_general(q4, ks, (((1,), (1,)), ((), ())), preferred_element_type=jnp.float32)
            lg = jnp.where(mask, lg, -jnp.inf)
            m = jnp.max(lg, axis=-1, keepdims=True)
            e = jnp.exp(lg - m)
            den = jnp.sum(e, axis=-1, keepdims=True)
            pv = jnp.dot(e.astype(vs.dtype), vs, preferred_element_type=jnp.float32) * (1.0 / den)
            lse = m + jnp.log(den)
            o_acc = pv[0:BLOCK]
            lse_acc = jnp.broadcast_to(lse[0:BLOCK], (BLOCK, w))
            for hh in range(1, B_HPG):
                hm = head_of_lane == hh
                o_acc = jnp.where(hm, pv[hh * BLOCK:(hh + 1) * BLOCK], o_acc)
                lse_acc = jnp.where(hm, lse[hh * BLOCK:(hh + 1) * BLOCK], lse_acc)
            o_ref[0, r, sb * BLOCK:(sb + 1) * BLOCK, :] = o_acc
            lse_ref[0, r, sb * BLOCK:(sb + 1) * BLOCK, :] = lse_acc


def _dilated_group(bq, bk, bv):
    B, dil, ld, w = bq.shape
    tqb = min(512, ld)
    per = tqb // BLOCK
    rps = min(dil, 512 // tqb)
    cur = pl.BlockSpec((1, rps, tqb, w), lambda b, r, i: (b, r, i, 0))
    prev = pl.BlockSpec((1, rps, BLOCK, w), lambda b, r, i: (b, r, jnp.maximum(i * per - 1, 0), 0))
    return pl.pallas_call(
        _dilated_kernel,
        grid=(B, dil // rps, ld // tqb),
        in_specs=[cur, cur, prev, cur, prev],
        out_specs=(cur, cur),
        out_shape=(jax.ShapeDtypeStruct((B, dil, ld, w), jnp.float32),) * 2,
        compiler_params=_cparams(3),
        name=f"dilated{dil}",
    )(bq, bk, bk, bv, bv)


def _sinkwin_kernel(sink_ref, q_ref, kc_ref, kp_ref, vc_ref, vp_ref, o_ref):
    tqb = q_ref.shape[1]
    ib = pl.program_id(1)
    q = q_ref[0]
    kcat = jnp.concatenate([kp_ref[0], kc_ref[0]], axis=0)
    vcat = jnp.concatenate([vp_ref[0], vc_ref[0]], axis=0)
    kv_of_lane = lax.broadcasted_iota(jnp.int32, (1, LANES), 1) // HEAD_DIM
    rr = lax.broadcasted_iota(jnp.int32, (BLOCK, 2 * BLOCK), 0)
    cc = lax.broadcasted_iota(jnp.int32, (BLOCK, 2 * BLOCK), 1)
    band = (cc > rr) & (cc <= rr + C_WINDOW)
    for sb in range(tqb // BLOCK):
        ks = kcat[sb * BLOCK:sb * BLOCK + 2 * BLOCK]
        vs = vcat[sb * BLOCK:sb * BLOCK + 2 * BLOCK]
        mask = band
        if sb == 0:
            mask = band & ((cc >= BLOCK) | (ib > 0))
        for g in range(C_GRP):
            qs = q[sb * BLOCK:(sb + 1) * BLOCK, g * LANES:(g + 1) * LANES]
            o_acc = jnp.zeros((BLOCK, LANES), jnp.float32)
            for kk in range(C_KV_HEADS):
                hm = kv_of_lane == kk
                sink = sink_ref[kk * C_GRP + g]
                qm = jnp.where(hm, qs, jnp.zeros_like(qs))
                lg = lax.dot_general(qm, ks, (((1,), (1,)), ((), ())), preferred_element_type=jnp.float32)
                lg = jnp.where(mask, lg, -jnp.inf)
                m = jnp.maximum(jnp.max(lg, axis=-1, keepdims=True), sink)
                e = jnp.exp(lg - m)
                den = jnp.sum(e, axis=-1, keepdims=True) + jnp.exp(sink - m)
                pv = jnp.dot(e.astype(vs.dtype), vs, preferred_element_type=jnp.float32)
                o_acc = jnp.where(hm, pv * (1.0 / den), o_acc)
            o_ref[0, sb * BLOCK:(sb + 1) * BLOCK, g * LANES:(g + 1) * LANES] = o_acc.astype(o_ref.dtype)


def _sinkwin(cq, ck, cv, sinks):
    B, L, _ = cq.shape
    tqb = min(512, L)
    per = tqb // BLOCK
    cur = lambda w: pl.BlockSpec((1, tqb, w), lambda b, i: (b, i, 0))
    prev = lambda w: pl.BlockSpec((1, BLOCK, w), lambda b, i: (b, jnp.maximum(i * per - 1, 0), 0))
    return pl.pallas_call(
        _sinkwin_kernel,
        grid=(B, L // tqb),
        in_specs=[pl.BlockSpec(memory_space=pltpu.SMEM),
                  cur(_W_CQ), cur(_W_CKV), prev(_W_CKV), cur(_W_CKV), prev(_W_CKV)],
        out_specs=cur(_W_CQ),
        out_shape=jax.ShapeDtypeStruct((B, L, _W_CQ), jnp.bfloat16),
        compiler_params=_cparams(2),
        name="sinkwin",
    )(sinks, cq, ck, ck, cv, cv)


def _mixer_out_kernel(h_ref, g_ref, wg_ref, oa_ref, ob0_ref, ob1_ref, ob2_ref, l0_ref, l1_ref, l2_ref,
                      oc_ref, wa_ref, wb_ref, wc_ref, wo_ref, out_ref, *stage_refs):
    tm = h_ref.shape[1]
    h = h_ref[0]
    u = _rmsnorm_f32(h, g_ref[...]).astype(jnp.bfloat16)

    def token_major(ref, stage_ref):
        dil = ref.shape[1]
        for r in range(dil):
            for t in range(B_GW // LANES):
                stage_ref[t, pl.ds(r, tm // dil, stride=dil), :] = ref[0, r, :, t * LANES:(t + 1) * LANES]
        return jnp.concatenate([stage_ref[t] for t in range(B_GW // LANES)], axis=1)

    ob0, l0 = ob0_ref[0, 0], l0_ref[0, 0]
    ob1, l1 = token_major(ob1_ref, stage_refs[0]), token_major(l1_ref, stage_refs[1])
    ob2, l2 = token_major(ob2_ref, stage_refs[2]), token_major(l2_ref, stage_refs[3])
    mx = jnp.maximum(jnp.maximum(l0, l1), l2)
    e0, e1, e2 = jnp.exp(l0 - mx), jnp.exp(l1 - mx), jnp.exp(l2 - mx)
    ob = (e0 * ob0 + e1 * ob1 + e2 * ob2) * (1.0 / (e0 + e1 + e2))

    def gate(i):
        z = jnp.dot(u, wg_ref[:, i * D_MODEL:(i + 1) * D_MODEL], preferred_element_type=jnp.float32)
        return jax.nn.sigmoid(z)

    merged = gate(0) * jnp.dot(oa_ref[0], wa_ref[...], preferred_element_type=jnp.float32)
    merged = merged + gate(1) * jnp.dot(ob.astype(jnp.bfloat16), wb_ref[...], preferred_element_type=jnp.float32)
    merged = merged + gate(2) * jnp.dot(oc_ref[0], wc_ref[...], preferred_element_type=jnp.float32)
    out_ref[0] = h + jnp.dot(merged.astype(jnp.bfloat16), wo_ref[...], preferred_element_type=jnp.float32)


def _mixer_out(h, g, wg, oa, obs, lses, oc, wa, wb, wc, wo):
    B, L, D = h.shape
    tm = TM
    tok = lambda w: pl.BlockSpec((1, tm, w), lambda b, i: (b, i, 0))
    full = lambda a: pl.BlockSpec(a.shape, lambda b, i: (0, 0))
    grp = lambda a: pl.BlockSpec((1, a.shape[1], tm // a.shape[1], B_GW), lambda b, i: (b, 0, i, 0))
    return pl.pallas_call(
        _mixer_out_kernel,
        grid=(B, L // tm),
        in_specs=[tok(D), full(g), full(wg), tok(_W_AQ)] + [grp(a) for a in obs] + [grp(a) for a in lses]
                 + [tok(_W_CQ), full(wa), full(wb), full(wc), full(wo)],
        out_specs=tok(D),
        out_shape=jax.ShapeDtypeStruct((B, L, D), jnp.float32),
        scratch_shapes=[pltpu.VMEM((B_GW // LANES, tm, LANES), jnp.float32)] * 4,
        compiler_params=_cparams(2),
        name="mixer_out",
    )(h, g, wg, oa, *obs, *lses, oc, wa, wb, wc, wo)


def _ffn_kernel(h_ref, g_ref, wup_ref, wdn_ref, gf_ref, out_ref, *, final, chunk):
    h = h_ref[...]
    u = _rmsnorm_f32(h, g_ref[...]).astype(jnp.bfloat16)
    acc = h
    for c in range(D_FF // chunk):
        up = jnp.dot(u, wup_ref[:, c * chunk:(c + 1) * chunk], preferred_element_type=jnp.float32)
        act = jnp.square(jnp.maximum(up, 0.0)).astype(jnp.bfloat16)
        acc = acc + jnp.dot(act, wdn_ref[c * chunk:(c + 1) * chunk, :], preferred_element_type=jnp.float32)
    if final:
        acc = _rmsnorm_f32(acc, gf_ref[...])
    out_ref[...] = acc


def _ffn(h2d, g, wup, wdn, gf, final):
    N, D = h2d.shape
    tm = TM
    row = pl.BlockSpec((tm, D), lambda i: (i, 0))
    full = lambda a: pl.BlockSpec(a.shape, lambda i: (0, 0))
    return pl.pallas_call(
        functools.partial(_ffn_kernel, final=final, chunk=1024),
        grid=(N // tm,),
        in_specs=[row, full(g), full(wup), full(wdn), full(gf)],
        out_specs=row,
        out_shape=jax.ShapeDtypeStruct((N, D), jnp.float32),
        compiler_params=_cparams(1),
        name="ffn",
    )(h2d, g, wup, wdn, gf)


def _rope_tables(positions):
    inv = ROPE_THETA ** (-jnp.arange(HALF, dtype=jnp.float32) / HALF)
    ang = positions.astype(jnp.float32)[..., None] * inv
    c, s = jnp.cos(ang), jnp.sin(ang)
    reps = LANES // HEAD_DIM
    cos = jnp.tile(jnp.concatenate([c, c], axis=-1), (1, 1, reps))
    sin = jnp.tile(jnp.concatenate([-s, s], axis=-1), (1, 1, reps))
    return cos, sin, jnp.swapaxes(c, 1, 2), jnp.swapaxes(s, 1, 2)


def _layer_weights(w_in, idx_k_norm, w_c):
    col = lambda n: w_in[:, _OFF[n][0]:_OFF[n][1]]
    scale = HEAD_DIM ** -0.5
    idx_scale = IDX_DIM ** -0.5
    cq = col("cq").reshape(D_MODEL, C_KV_HEADS, C_GRP, HEAD_DIM).transpose(0, 2, 1, 3).reshape(D_MODEL, _W_CQ)
    wt = jnp.concatenate([col("ak"), col("ik"), col("bq") * scale, col("bk"), col("bv"),
                          cq * scale, col("ck"), col("cv")], axis=1).astype(jnp.bfloat16)
    wT = jnp.concatenate([col("aq") * (scale * LOG2E), col("iq") * idx_scale, col("av"), col("iw")],
                         axis=1).T.astype(jnp.bfloat16)
    wg = col("gates").astype(jnp.bfloat16)
    gk = jnp.concatenate([jnp.ones((IDX_DIM,), jnp.float32), idx_k_norm])[None, :]
    wc = w_c.reshape(C_KV_HEADS, C_GRP, HEAD_DIM, D_MODEL).transpose(1, 0, 2, 3).reshape(_W_CQ, D_MODEL)
    return wt, wT, wg, gk, wc.astype(jnp.bfloat16)


def kernel(x, positions, attn_norm, w_in, idx_k_norm, sinks, w_a, w_b, w_c, w_o, mlp_norm, w_up, w_down, final_norm):
    B, L, D = x.shape
    depth = w_in.shape[0]
    assert D == D_MODEL and L % (16 * BLOCK) == 0
    tabs = _rope_tables(positions)
    bf = jnp.bfloat16
    h = x
    for l in range(depth):
        wt, wT, wg, gk, wc = _layer_weights(w_in[l], idx_k_norm[l], w_c[l])
        (ka, ik, bq0, bq1, bq2, bk0, bk1, bk2, bv0, bv1, bv2, cq, ck, cv, qa, iq, va, iw) = _inproj(
            h, attn_norm[l][None, :], wt, wT, tabs, gk)
        oa = _dsa(ik, ka, va, iq, qa, iw)
        obs, lses = [], []
        for q_, k_, v_ in ((bq0, bk0, bv0), (bq1, bk1, bv1), (bq2, bk2, bv2)):
            o, lse = _dilated_group(q_, k_, v_)
            obs.append(o)
            lses.append(lse)
        oc = _sinkwin(cq, ck, cv, sinks[l])
        h2 = _mixer_out(h, attn_norm[l][None, :], wg, oa, obs, lses, oc,
                        w_a[l].astype(bf), w_b[l].astype(bf), wc, w_o[l].astype(bf))
        h3 = _ffn(h2.reshape(B * L, D), mlp_norm[l][None, :], w_up[l].astype(bf), w_down[l].astype(bf),
                  final_norm[None, :], l == depth - 1)
        h = h3.reshape(B, L, D)
    return h
```

```python
import functools

import jax
import jax.numpy as jnp
from jax import lax
from jax.experimental import pallas as pl
from jax.experimental.pallas import tpu as pltpu

D_MODEL = 1024
HEAD_DIM = 64
HALF = HEAD_DIM // 2
ROPE_THETA = 10000.0
NORM_EPS = 1e-6
A_HEADS = 6
IDX_HEADS = 8
IDX_DIM = 64
TOPK_MAX = 256
B_GROUPS = ((128, 1), (512, 4), (2048, 16))
B_HPG = 4
B_HEADS = B_HPG * len(B_GROUPS)
B_GW = B_HPG * HEAD_DIM
B_KEYS = 128
C_HEADS = 8
C_KV_HEADS = 2
C_GRP = C_HEADS // C_KV_HEADS
C_WINDOW = 128
N_BRANCHES = 3
D_FF = 4 * D_MODEL

BLOCK = 128
LANES = 128
INT_MIN = -2 ** 31
I16_MIN = -2 ** 15
VMEM_LIMIT = 56 * 1024 * 1024

_W_AQ = A_HEADS * HEAD_DIM
_W_IQ = IDX_HEADS * IDX_DIM
_W_B = B_HEADS * HEAD_DIM
_W_CQ = C_HEADS * HEAD_DIM
_W_CKV = C_KV_HEADS * HEAD_DIM
_OFF = {}
_o = 0
for _name, _w in (("aq", _W_AQ), ("ak", HEAD_DIM), ("av", HEAD_DIM), ("iq", _W_IQ), ("ik", IDX_DIM),
                  ("iw", IDX_HEADS), ("bq", _W_B), ("bk", _W_B), ("bv", _W_B), ("cq", _W_CQ),
                  ("ck", _W_CKV), ("cv", _W_CKV), ("gates", N_BRANCHES * D_MODEL)):
    _OFF[_name] = (_o, _o + _w)
    _o += _w

KC = 256
DEN_ROWS = 16
LOG2E = 1.4426950408889634
TM = 512


def _rmsnorm_f32(x, g):
    return x * lax.rsqrt(jnp.mean(x * x, axis=-1, keepdims=True) + NORM_EPS) * g


def _cparams(n_axes):
    return pltpu.CompilerParams(dimension_semantics=("arbitrary",) * n_axes, vmem_limit_bytes=VMEM_LIMIT)


def _inproj_kernel(x_ref, g_ref, wt_ref, wT_ref, cos_ref, sin_ref, cosT_ref, sinT_ref, gk_ref,
                   ka_ref, ik_ref,
                   bq0_ref, bq1_ref, bq2_ref, bk0_ref, bk1_ref, bk2_ref, bv0_ref, bv1_ref, bv2_ref,
                   cq_ref, ck_ref, cv_ref, qa_ref, iq_ref, va_ref, iw_ref, stage_ref):
    tm = x_ref.shape[1]
    u = _rmsnorm_f32(x_ref[0], g_ref[...]).astype(jnp.bfloat16)
    cos = cos_ref[0]
    sin = sin_ref[0]
    lane = lax.broadcasted_iota(jnp.int32, (1, LANES), 1)
    first_half = (lane % HEAD_DIM) < HALF

    def rope_tile(z):
        rot = jnp.where(first_half, pltpu.roll(z, LANES - HALF, 1), pltpu.roll(z, HALF, 1))
        return z * cos + rot * sin

    def project(c0, width):
        return jnp.dot(u, wt_ref[:, c0:c0 + width], preferred_element_type=jnp.float32)

    c0 = 0
    z = project(c0, 2 * HEAD_DIM)
    is_ik = lane >= HEAD_DIM
    ms = jnp.sum(jnp.where(is_ik, z * z, 0.0), axis=-1, keepdims=True) * (1.0 / IDX_DIM)
    z = z * jnp.where(is_ik, lax.rsqrt(ms + NORM_EPS) * gk_ref[...], 1.0)
    z = rope_tile(z)
    ka_ref[0] = z[:, :HEAD_DIM].astype(ka_ref.dtype)
    ik_ref[0] = z[:, HEAD_DIM:].astype(ik_ref.dtype)
    c0 += 2 * HEAD_DIM

    for refs, roped in (((bq0_ref, bq1_ref, bq2_ref), True), ((bk0_ref, bk1_ref, bk2_ref), True),
                        ((bv0_ref, bv1_ref, bv2_ref), False)):
        for g, (ref, (_, dil)) in enumerate(zip(refs, B_GROUPS)):
            z = project(c0 + g * B_GW, B_GW)
            if roped:
                z = jnp.concatenate([rope_tile(z[:, t * LANES:(t + 1) * LANES]) for t in range(B_GW // LANES)], axis=1)
            if dil == 1:
                ref[0, 0] = z.astype(ref.dtype)
            else:
                for t in range(B_GW // LANES):
                    stage_ref[t] = z[:, t * LANES:(t + 1) * LANES]
                for r in range(dil):
                    for t in range(B_GW // LANES):
                        ref[0, r, :, t * LANES:(t + 1) * LANES] = (
                            stage_ref[t, pl.ds(r, tm // dil, stride=dil), :].astype(ref.dtype))
        c0 += _W_B

    for ref, width, roped in ((cq_ref, _W_CQ, True), (ck_ref, _W_CKV, True), (cv_ref, _W_CKV, False)):
        z = project(c0, width)
        if roped:
            for t in range(width // LANES):
                ref[0, :, t * LANES:(t + 1) * LANES] = rope_tile(z[:, t * LANES:(t + 1) * LANES]).astype(ref.dtype)
        else:
            ref[0] = z.astype(ref.dtype)
        c0 += width

    def project_t(r0, rows):
        return lax.dot_general(wT_ref[r0:r0 + rows, :], u, (((1,), (1,)), ((), ())),
                               preferred_element_type=jnp.float32)

    cosT = cosT_ref[0]
    sinT = sinT_ref[0]

    def rope_rows(zt, ref, heads):
        for h in range(heads):
            x1 = zt[h * HEAD_DIM:h * HEAD_DIM + HALF]
            x2 = zt[h * HEAD_DIM + HALF:(h + 1) * HEAD_DIM]
            o1 = (x1 * cosT - x2 * sinT).astype(ref.dtype)
            o2 = (x2 * cosT + x1 * sinT).astype(ref.dtype)
            for c in range(tm // KC):
                ref[0, c, 0:HALF, h * KC:(h + 1) * KC] = o1[:, c * KC:(c + 1) * KC]
                ref[0, c, HALF:HEAD_DIM, h * KC:(h + 1) * KC] = o2[:, c * KC:(c + 1) * KC]

    zt = project_t(0, wT_ref.shape[0])
    r0 = 0
    rope_rows(zt[r0:r0 + _W_AQ], qa_ref, A_HEADS)
    r0 += _W_AQ
    rope_rows(zt[r0:r0 + _W_IQ], iq_ref, IDX_HEADS)
    r0 += _W_IQ
    zv = zt[r0:r0 + HEAD_DIM]
    for c in range(tm // KC):
        va_ref[0, c] = zv[:, c * KC:(c + 1) * KC].astype(va_ref.dtype)
    r0 += HEAD_DIM
    zw = zt[r0:r0 + IDX_HEADS] * (IDX_HEADS ** -0.5)
    for c in range(tm // KC):
        for h in range(IDX_HEADS):
            iw_ref[0, c, :, h * KC:(h + 1) * KC] = zw[h:h + 1, c * KC:(c + 1) * KC]


def _inproj(h, g, wt, wT, tabs, gk):
    B, L, D = h.shape
    tm = TM
    nq = L // KC
    bf = jnp.bfloat16
    tok = lambda w: pl.BlockSpec((1, tm, w), lambda b, i: (b, i, 0))
    tr = lambda r: pl.BlockSpec((1, r, tm), lambda b, i: (b, 0, i))
    full2 = lambda a: pl.BlockSpec(a.shape, lambda b, i: (0, 0))
    qtile = lambda rows, w: pl.BlockSpec((1, tm // KC, rows, w), lambda b, i: (b, i, 0, 0))
    grp_shape = lambda dil: jax.ShapeDtypeStruct((B, dil, L // dil, B_GW), bf)
    grp_spec = lambda dil: pl.BlockSpec((1, dil, tm // dil, B_GW), lambda b, i: (b, 0, i, 0))
    dils = [d for _, d in B_GROUPS]
    out_shape = (
        [jax.ShapeDtypeStruct((B, L, HEAD_DIM), bf), jax.ShapeDtypeStruct((B, L, IDX_DIM), bf)]
        + [grp_shape(d) for d in dils] * 3
        + [jax.ShapeDtypeStruct((B, L, _W_CQ), bf), jax.ShapeDtypeStruct((B, L, _W_CKV), bf),
           jax.ShapeDtypeStruct((B, L, _W_CKV), bf),
           jax.ShapeDtypeStruct((B, nq, HEAD_DIM, A_HEADS * KC), bf),
           jax.ShapeDtypeStruct((B, nq, IDX_DIM, IDX_HEADS * KC), bf),
           jax.ShapeDtypeStruct((B, nq, HEAD_DIM, KC), bf),
           jax.ShapeDtypeStruct((B, nq, 1, IDX_HEADS * KC), jnp.float32)])
    out_specs = (
        [tok(HEAD_DIM), tok(IDX_DIM)] + [grp_spec(d) for d in dils] * 3
        + [tok(_W_CQ), tok(_W_CKV), tok(_W_CKV),
           qtile(HEAD_DIM, A_HEADS * KC), qtile(IDX_DIM, IDX_HEADS * KC), qtile(HEAD_DIM, KC),
           qtile(1, IDX_HEADS * KC)])
    cos, sin, cosT, sinT = tabs
    return pl.pallas_call(
        _inproj_kernel,
        grid=(B, L // tm),
        in_specs=[tok(D), full2(g), full2(wt), full2(wT), tok(LANES), tok(LANES), tr(HALF), tr(HALF), full2(gk)],
        out_specs=out_specs,
        out_shape=out_shape,
        scratch_shapes=[pltpu.VMEM((B_GW // LANES, tm, LANES), jnp.float32)],
        compiler_params=_cparams(2),
        name="inproj",
    )(h, g, wt, wT, cos, sin, cosT, sinT, gk)


def _dsa_kernel(ik_ref, ka_ref, va_ref, iq_ref, qa_ref, iw_ref, o_ref,
                key_ref, hi_ref, lo_ref, r_ref, lg_ref, cm_ref, p_ref, m_ref, alpha_ref, acc_ref, outT_ref,
                *, topk):
    tq = o_ref.shape[1]
    qi = pl.program_id(1)
    nch = qi + 1
    npair = (nch + 1) // 2
    nfull = nch // 2
    last = nch - 1
    col = qi * tq + lax.broadcasted_iota(jnp.int32, (KC, tq), 1)
    row0 = lax.broadcasted_iota(jnp.int32, (KC, tq), 0)
    f32 = jnp.float32

    def score_dot(j):
        s0 = pl.multiple_of(j * KC, KC)
        return jnp.dot(ik_ref[0, pl.ds(s0, KC), :], iq_ref[0, 0], preferred_element_type=f32)

    r_ref[0] = score_dot(0)

    def score_chunk(j, slot, issue_next=True):
        s0 = pl.multiple_of(j * KC, KC)
        if issue_next:
            r_ref[1 - slot] = score_dot(jnp.minimum(j + 1, last))
        wr = jnp.maximum(r_ref[slot], 0.0) * iw_ref[0, 0]
        acc = wr[:, 0:tq]
        for h in range(1, IDX_HEADS):
            acc = acc + wr[:, h * tq:(h + 1) * tq]
        bits = lax.bitcast_convert_type(acc, jnp.int32)
        key = jnp.where(bits < 0, INT_MIN - bits, bits)
        key = jnp.where(row0 + s0 <= col, key, INT_MIN)
        key_ref[pl.ds(s0, KC), :] = key
        hi_ref[pl.ds(s0, KC), :] = lax.shift_right_arithmetic(key, 16).astype(jnp.int16)

    def score_pair(j2, carry):
        score_chunk(2 * j2, 0)
        score_chunk(2 * j2 + 1, 1)
        return carry

    lax.fori_loop(0, nfull, score_pair, 0)

    @pl.when(nch % 2 == 1)
    def _():
        score_chunk(last, 0, issue_next=False)
        pad0 = pl.multiple_of(nch * KC, KC)
        hi_ref[pl.ds(pad0, KC), :] = jnp.full((KC, tq), I16_MIN, jnp.int16)
        lo_ref[pl.ds(pad0, KC), :] = jnp.full((KC, tq), I16_MIN, jnp.int16)

    one_b = jnp.ones((), jnp.bfloat16)
    zero_b = jnp.zeros((), jnp.bfloat16)
    rows = 2 * KC
    nacc = 4

    def count16(ref, cand16):
        def cnt_body(j, c):
            s0 = pl.multiple_of(j * rows, rows)
            w = jnp.where(ref[pl.ds(s0, rows), :] >= cand16, one_b, zero_b)
            accs = [w[a * 16:(a + 1) * 16] for a in range(nacc)]
            for i in range(nacc, rows // 16):
                accs[i % nacc] = accs[i % nacc] + w[i * 16:(i + 1) * 16]
            return c + ((accs[0] + accs[1]) + (accs[2] + accs[3])).astype(f32)
        c16 = lax.fori_loop(0, npair, cnt_body, jnp.zeros((16, tq), f32))
        return jnp.sum(c16, axis=0, keepdims=True)

    def bit_step(ref, k_want, shift, st):
        t_u, cnt, c_rej = st
        cand_u = t_u | lax.shift_left(jnp.int32(1), shift)
        c = count16(ref, (cand_u - 32768).astype(jnp.int16))
        take = c >= k_want
        return jnp.where(take, cand_u, t_u), jnp.where(take, c, cnt), jnp.where(take, c_rej, c)

    def search16(ref, k_want, cnt0, active, early_exit):
        st0 = (jnp.zeros((1, tq), jnp.int32), cnt0, jnp.zeros((1, tq), f32))
        if not early_exit:
            return lax.fori_loop(0, 16, lambda i, st: bit_step(ref, k_want, 15 - i, st), st0)

        def cond(st):
            g, (_, cnt, _) = st
            pending = jnp.logical_and(active, cnt != k_want)
            return jnp.logical_and(g < 4, jnp.max(pending.astype(f32)) > 0.0)

        def body(st):
            g, inner = st
            for k in range(4):
                inner = bit_step(ref, k_want, 15 - (4 * g + k), inner)
            return g + 1, inner

        return lax.while_loop(cond, body, (jnp.int32(0), st0))[1]

    n_valid = (qi * tq + lax.broadcasted_iota(jnp.int32, (1, tq), 1) + 1).astype(f32)
    kf = jnp.full((1, tq), float(topk), f32)
    active = n_valid > kf
    th_u, cnt_h, gt_h = search16(hi_ref, kf, n_valid, active, early_exit=False)
    th16 = (th_u - 32768).astype(jnp.int16)

    def lo_body(j, carry):
        s0 = pl.multiple_of(j * KC, KC)
        lo = ((key_ref[pl.ds(s0, KC), :] & 0xFFFF) - 32768).astype(jnp.int16)
        lo_ref[pl.ds(s0, KC), :] = jnp.where(hi_ref[pl.ds(s0, KC), :] == th16, lo, jnp.full_like(lo, I16_MIN))
        return carry

    lax.fori_loop(0, nch, lo_body, 0)
    active_lo = jnp.logical_and(active, cnt_h != kf)
    k_lo = jnp.where(active_lo, kf - gt_h, jnp.inf)
    tl_u, cnt_l, gt_l = search16(lo_ref, k_lo, cnt_h - gt_h, active_lo, early_exit=True)
    t32 = lax.shift_left(th_u - 32768, 16) | tl_u
    thr = jnp.maximum(t32, INT_MIN + 1)

    tied = jnp.logical_and(active_lo, cnt_l > k_lo)

    @pl.when(jnp.max(tied.astype(f32)) > 0.0)
    def _():
        need = k_lo - gt_l
        tri = (lax.broadcasted_iota(jnp.int32, (KC, KC), 1)
               < lax.broadcasted_iota(jnp.int32, (KC, KC), 0)).astype(jnp.bfloat16)

        def tie_body(j, run):
            s0 = pl.multiple_of(j * KC, KC)
            k = key_ref[pl.ds(s0, KC), :]
            eq = k == thr
            before = run + jnp.dot(tri, eq.astype(jnp.bfloat16), preferred_element_type=f32)
            drop = jnp.logical_and(jnp.logical_and(eq, tied), before >= need)
            key_ref[pl.ds(s0, KC), :] = jnp.where(drop, INT_MIN, k)
            return run + jnp.sum(eq.astype(f32), axis=0, keepdims=True)

        lax.fori_loop(0, nch, tie_body, jnp.zeros((1, tq), f32))

    def logits_stage(j, slot):
        s0 = pl.multiple_of(j * KC, KC)
        lg = jnp.dot(ka_ref[0, pl.ds(s0, KC), :], qa_ref[0, 0], preferred_element_type=f32)
        bias = jnp.where(key_ref[pl.ds(s0, KC), :] >= thr, 0.0, -jnp.inf)
        for h in range(A_HEADS):
            lgh = lg[:, h * tq:(h + 1) * tq] + bias
            lg_ref[slot, :, h * tq:(h + 1) * tq] = lgh
            cm_ref[slot, :, h * tq:(h + 1) * tq] = jnp.max(lgh, axis=0, keepdims=True)

    m_ref[...] = jnp.full(m_ref.shape, -1e30, f32)
    alpha_ref[...] = jnp.ones(alpha_ref.shape, f32)
    acc_ref[...] = jnp.zeros(acc_ref.shape, f32)
    p_ref[1] = jnp.zeros(p_ref.shape[1:], p_ref.dtype)
    logits_stage(0, 0)
    ones_rows = (lax.broadcasted_iota(jnp.int32, (DEN_ROWS, KC), 0) == 0).astype(jnp.bfloat16)

    def value_product(j, slot):
        v_ext = jnp.concatenate([va_ref[0, j], ones_rows], axis=0)
        return jnp.dot(v_ext, p_ref[slot], preferred_element_type=f32)

    def att_chunk(jnext, jprev, slot):
        if jnext is not None:
            logits_stage(jnext, 1 - slot)
        acc_ref[...] = alpha_ref[...] * acc_ref[...] + value_product(jprev, 1 - slot)
        m_old = m_ref[...]
        m_new = jnp.maximum(m_old, cm_ref[slot])
        for h in range(A_HEADS):
            ph = jnp.exp2(lg_ref[slot, :, h * tq:(h + 1) * tq] - m_new[:, h * tq:(h + 1) * tq])
            p_ref[slot, :, h * tq:(h + 1) * tq] = ph.astype(p_ref.dtype)
        alpha_ref[...] = jnp.exp2(m_old - m_new)
        m_ref[...] = m_new

    def att_pair(j2, carry):
        ja = 2 * j2
        att_chunk(ja + 1, jnp.maximum(ja - 1, 0), 0)
        att_chunk(jnp.minimum(ja + 2, last), ja, 1)
        return carry

    lax.fori_loop(0, nfull, att_pair, 0)

    def finish(slot):
        acc = alpha_ref[...] * acc_ref[...] + value_product(last, slot)
        out = acc[0:HEAD_DIM] * (1.0 / acc[HEAD_DIM:HEAD_DIM + 1])
        for h in range(A_HEADS):
            outT_ref[h * HEAD_DIM:(h + 1) * HEAD_DIM, :] = out[:, h * tq:(h + 1) * tq]
        o_ref[0] = outT_ref[...].T.astype(o_ref.dtype)

    @pl.when(nch % 2 == 1)
    def _():
        att_chunk(None, jnp.maximum(last - 1, 0), 0)
        finish(0)

    @pl.when(nch % 2 == 0)
    def _():
        finish(1)


def _dsa(ik, ka, va, iq, qa, iw):
    B, L, _ = ik.shape
    tq = KC
    topk = min(TOPK_MAX, L // 4)
    seq = lambda w: pl.BlockSpec((1, L, w), lambda b, i: (b, 0, 0))
    qtile = lambda rows, w: pl.BlockSpec((1, 1, rows, w), lambda b, i: (b, i, 0, 0))
    f32 = jnp.float32
    return pl.pallas_call(
        functools.partial(_dsa_kernel, topk=topk),
        grid=(B, L // tq),
        in_specs=[seq(IDX_DIM), seq(HEAD_DIM),
                  pl.BlockSpec((1, L // KC, HEAD_DIM, KC), lambda b, i: (b, 0, 0, 0)),
                  qtile(IDX_DIM, IDX_HEADS * tq), qtile(HEAD_DIM, A_HEADS * tq), qtile(1, IDX_HEADS * tq)],
        out_specs=pl.BlockSpec((1, tq, _W_AQ), lambda b, i: (b, i, 0)),
        out_shape=jax.ShapeDtypeStruct((B, L, _W_AQ), jnp.bfloat16),
        scratch_shapes=[pltpu.VMEM((L, tq), jnp.int32),
                        pltpu.VMEM((L + KC, tq), jnp.int16),
                        pltpu.VMEM((L + KC, tq), jnp.int16),
                        pltpu.VMEM((2, KC, IDX_HEADS * tq), f32),
                        pltpu.VMEM((2, KC, A_HEADS * tq), f32),
                        pltpu.VMEM((2, 1, A_HEADS * tq), f32),
                        pltpu.VMEM((2, KC, A_HEADS * tq), jnp.bfloat16),
                        pltpu.VMEM((1, A_HEADS * tq), f32),
                        pltpu.VMEM((1, A_HEADS * tq), f32),
                        pltpu.VMEM((HEAD_DIM + DEN_ROWS, A_HEADS * tq), f32),
                        pltpu.VMEM((_W_AQ, tq), f32)],
        compiler_params=_cparams(2),
        name="dsa",
    )(ik, ka, va, iq, qa, iw)


def _dilated_kernel(q_ref, kc_ref, kp_ref, vc_ref, vp_ref, o_ref, lse_ref):
    tqb = q_ref.shape[2]
    w = q_ref.shape[3]
    ib = pl.program_id(2)
    head_of_lane = lax.broadcasted_iota(jnp.int32, (1, w), 1) // HEAD_DIM
    nrow = B_HPG * BLOCK
    rr = lax.broadcasted_iota(jnp.int32, (nrow, 2 * BLOCK), 0) % BLOCK
    cc = lax.broadcasted_iota(jnp.int32, (nrow, 2 * BLOCK), 1)
    band = (cc >= rr) & (cc <= rr + B_KEYS)
    for r in range(q_ref.shape[1]):
        q = q_ref[0, r]
        kcat = jnp.concatenate([kp_ref[0, r], kc_ref[0, r]], axis=0)
        vcat = jnp.concatenate([vp_ref[0, r], vc_ref[0, r]], axis=0)
        for sb in range(tqb // BLOCK):
            qs = q[sb * BLOCK:(sb + 1) * BLOCK]
            ks = kcat[sb * BLOCK:sb * BLOCK + 2 * BLOCK]
            vs = vcat[sb * BLOCK:sb * BLOCK + 2 * BLOCK]
            mask = band
            if sb == 0:
                mask = band & ((cc >= BLOCK) | (ib > 0))
            q4 = jnp.concatenate([jnp.where(head_of_lane == hh, qs, jnp.zeros_like(qs)) for hh in range(B_HPG)],
                                 axis=0)
            lg = lax.dot_general(q4, ks, (((1,), (1,)), ((), ())), preferred_element_type=jnp.float32)
            lg = jnp.where(mask, lg, -jnp.inf)
            m = jnp.max(lg, axis=-1, keepdims=True)
            e = jnp.exp(lg - m)
            den = jnp.sum(e, axis=-1, keepdims=True)
            pv = jnp.dot(e.astype(vs.dtype), vs, preferred_element_type=jnp.float32) * (1.0 / den)
            lse = m + jnp.log(den)
            o_acc = pv[0:BLOCK]
            lse_acc = jnp.broadcast_to(lse[0:BLOCK], (BLOCK, w))
            for hh in range(1, B_HPG):
                hm = head_of_lane == hh
                o_acc = jnp.where(hm, pv[hh * BLOCK:(hh + 1) * BLOCK], o_acc)
                lse_acc = jnp.where(hm, lse[hh * BLOCK:(hh + 1) * BLOCK], lse_acc)
            o_ref[0, r, sb * BLOCK:(sb + 1) * BLOCK, :] = o_acc
            lse_ref[0, r, sb * BLOCK:(sb + 1) * BLOCK, :] = lse_acc


def _dilated_group(bq, bk, bv):
    B, dil, ld, w = bq.shape
    tqb = min(512, ld)
    per = tqb // BLOCK
    rps = min(dil, 512 // tqb)
    cur = pl.BlockSpec((1, rps, tqb, w), lambda b, r, i: (b, r, i, 0))
    prev = pl.BlockSpec((1, rps, BLOCK, w), lambda b, r, i: (b, r, jnp.maximum(i * per - 1, 0), 0))
    return pl.pallas_call(
        _dilated_kernel,
        grid=(B, dil // rps, ld // tqb),
        in_specs=[cur, cur, prev, cur, prev],
        out_specs=(cur, cur),
        out_shape=(jax.ShapeDtypeStruct((B, dil, ld, w), jnp.float32),) * 2,
        compiler_params=_cparams(3),
        name=f"dilated{dil}",
    )(bq, bk, bk, bv, bv)


def _sinkwin_kernel(sink_ref, q_ref, kc_ref, kp_ref, vc_ref, vp_ref, o_ref):
    tqb = q_ref.shape[1]
    ib = pl.program_id(1)
    q = q_ref[0]
    kcat = jnp.concatenate([kp_ref[0], kc_ref[0]], axis=0)
    vcat = jnp.concatenate([vp_ref[0], vc_ref[0]], axis=0)
    kv_of_lane = lax.broadcasted_iota(jnp.int32, (1, LANES), 1) // HEAD_DIM
    rr = lax.broadcasted_iota(jnp.int32, (BLOCK, 2 * BLOCK), 0)
    cc = lax.broadcasted_iota(jnp.int32, (BLOCK, 2 * BLOCK), 1)
    band = (cc > rr) & (cc <= rr + C_WINDOW)
    for sb in range(tqb // BLOCK):
        ks = kcat[sb * BLOCK:sb * BLOCK + 2 * BLOCK]
        vs = vcat[sb * BLOCK:sb * BLOCK + 2 * BLOCK]
        mask = band
        if sb == 0:
            mask = band & ((cc >= BLOCK) | (ib > 0))
        for g in range(C_GRP):
            qs = q[sb * BLOCK:(sb + 1) * BLOCK, g * LANES:(g + 1) * LANES]
            o_acc = jnp.zeros((BLOCK, LANES), jnp.float32)
            for kk in range(C_KV_HEADS):
                hm = kv_of_lane == kk
                sink = sink_ref[kk * C_GRP + g]
                qm = jnp.where(hm, qs, jnp.zeros_like(qs))
                lg = lax.dot_general(qm, ks, (((1,), (1,)), ((), ())), preferred_element_type=jnp.float32)
                lg = jnp.where(mask, lg, -jnp.inf)
                m = jnp.maximum(jnp.max(lg, axis=-1, keepdims=True), sink)
                e = jnp.exp(lg - m)
                den = jnp.sum(e, axis=-1, keepdims=True) + jnp.exp(sink - m)
                pv = jnp.dot(e.astype(vs.dtype), vs, preferred_element_type=jnp.float32)
                o_acc = jnp.where(hm, pv * (1.0 / den), o_acc)
            o_ref[0, sb * BLOCK:(sb + 1) * BLOCK, g * LANES:(g + 1) * LANES] = o_acc.astype(o_ref.dtype)


def _sinkwin(cq, ck, cv, sinks):
    B, L, _ = cq.shape
    tqb = min(512, L)
    per = tqb // BLOCK
    cur = lambda w: pl.BlockSpec((1, tqb, w), lambda b, i: (b, i, 0))
    prev = lambda w: pl.BlockSpec((1, BLOCK, w), lambda b, i: (b, jnp.maximum(i * per - 1, 0), 0))
    return pl.pallas_call(
        _sinkwin_kernel,
        grid=(B, L // tqb),
        in_specs=[pl.BlockSpec(memory_space=pltpu.SMEM),
                  cur(_W_CQ), cur(_W_CKV), prev(_W_CKV), cur(_W_CKV), prev(_W_CKV)],
        out_specs=cur(_W_CQ),
        out_shape=jax.ShapeDtypeStruct((B, L, _W_CQ), jnp.bfloat16),
        compiler_params=_cparams(2),
        name="sinkwin",
    )(sinks, cq, ck, ck, cv, cv)


def _mixer_out_kernel(h_ref, g_ref, wg_ref, oa_ref, ob0_ref, ob1_ref, ob2_ref, l0_ref, l1_ref, l2_ref,
                      oc_ref, wa_ref, wb_ref, wc_ref, wo_ref, out_ref, *stage_refs):
    tm = h_ref.shape[1]
    h = h_ref[0]
    u = _rmsnorm_f32(h, g_ref[...]).astype(jnp.bfloat16)

    def token_major(ref, stage_ref):
        dil = ref.shape[1]
        for r in range(dil):
            for t in range(B_GW // LANES):
                stage_ref[t, pl.ds(r, tm // dil, stride=dil), :] = ref[0, r, :, t * LANES:(t + 1) * LANES]
        return jnp.concatenate([stage_ref[t] for t in range(B_GW // LANES)], axis=1)

    ob0, l0 = ob0_ref[0, 0], l0_ref[0, 0]
    ob1, l1 = token_major(ob1_ref, stage_refs[0]), token_major(l1_ref, stage_refs[1])
    ob2, l2 = token_major(ob2_ref, stage_refs[2]), token_major(l2_ref, stage_refs[3])
    mx = jnp.maximum(jnp.maximum(l0, l1), l2)
    e0, e1, e2 = jnp.exp(l0 - mx), jnp.exp(l1 - mx), jnp.exp(l2 - mx)
    ob = (e0 * ob0 + e1 * ob1 + e2 * ob2) * (1.0 / (e0 + e1 + e2))

    def gate(i):
        z = jnp.dot(u, wg_ref[:, i * D_MODEL:(i + 1) * D_MODEL], preferred_element_type=jnp.float32)
        return jax.nn.sigmoid(z)

    merged = gate(0) * jnp.dot(oa_ref[0], wa_ref[...], preferred_element_type=jnp.float32)
    merged = merged + gate(1) * jnp.dot(ob.astype(jnp.bfloat16), wb_ref[...], preferred_element_type=jnp.float32)
    merged = merged + gate(2) * jnp.dot(oc_ref[0], wc_ref[...], preferred_element_type=jnp.float32)
    out_ref[0] = h + jnp.dot(merged.astype(jnp.bfloat16), wo_ref[...], preferred_element_type=jnp.float32)


def _mixer_out(h, g, wg, oa, obs, lses, oc, wa, wb, wc, wo):
    B, L, D = h.shape
    tm = TM
    tok = lambda w: pl.BlockSpec((1, tm, w), lambda b, i: (b, i, 0))
    full = lambda a: pl.BlockSpec(a.shape, lambda b, i: (0, 0))
    grp = lambda a: pl.BlockSpec((1, a.shape[1], tm // a.shape[1], B_GW), lambda b, i: (b, 0, i, 0))
    return pl.pallas_call(
        _mixer_out_kernel,
        grid=(B, L // tm),
        in_specs=[tok(D), full(g), full(wg), tok(_W_AQ)] + [grp(a) for a in obs] + [grp(a) for a in lses]
                 + [tok(_W_CQ), full(wa), full(wb), full(wc), full(wo)],
        out_specs=tok(D),
        out_shape=jax.ShapeDtypeStruct((B, L, D), jnp.float32),
        scratch_shapes=[pltpu.VMEM((B_GW // LANES, tm, LANES), jnp.float32)] * 4,
        compiler_params=_cparams(2),
        name="mixer_out",
    )(h, g, wg, oa, *obs, *lses, oc, wa, wb, wc, wo)


def _ffn_kernel(h_ref, g_ref, wup_ref, wdn_ref, gf_ref, out_ref, *, final, chunk):
    h = h_ref[...]
    u = _rmsnorm_f32(h, g_ref[...]).astype(jnp.bfloat16)
    acc = h
    for c in range(D_FF // chunk):
        up = jnp.dot(u, wup_ref[:, c * chunk:(c + 1) * chunk], preferred_element_type=jnp.float32)
        act = jnp.square(jnp.maximum(up, 0.0)).astype(jnp.bfloat16)
        acc = acc + jnp.dot(act, wdn_ref[c * chunk:(c + 1) * chunk, :], preferred_element_type=jnp.float32)
    if final:
        acc = _rmsnorm_f32(acc, gf_ref[...])
    out_ref[...] = acc


def _ffn(h2d, g, wup, wdn, gf, final):
    N, D = h2d.shape
    tm = TM
    row = pl.BlockSpec((tm, D), lambda i: (i, 0))
    full = lambda a: pl.BlockSpec(a.shape, lambda i: (0, 0))
    return pl.pallas_call(
        functools.partial(_ffn_kernel, final=final, chunk=1024),
        grid=(N // tm,),
        in_specs=[row, full(g), full(wup), full(wdn), full(gf)],
        out_specs=row,
        out_shape=jax.ShapeDtypeStruct((N, D), jnp.float32),
        compiler_params=_cparams(1),
        name="ffn",
    )(h2d, g, wup, wdn, gf)


def _rope_tables(positions):
    inv = ROPE_THETA ** (-jnp.arange(HALF, dtype=jnp.float32) / HALF)
    ang = positions.astype(jnp.float32)[..., None] * inv
    c, s = jnp.cos(ang), jnp.sin(ang)
    reps = LANES // HEAD_DIM
    cos = jnp.tile(jnp.concatenate([c, c], axis=-1), (1, 1, reps))
    sin = jnp.tile(jnp.concatenate([-s, s], axis=-1), (1, 1, reps))
    return cos, sin, jnp.swapaxes(c, 1, 2), jnp.swapaxes(s, 1, 2)


def _layer_weights(w_in, idx_k_norm, w_c):
    col = lambda n: w_in[:, _OFF[n][0]:_OFF[n][1]]
    scale = HEAD_DIM ** -0.5
    idx_scale = IDX_DIM ** -0.5
    cq = col("cq").reshape(D_MODEL, C_KV_HEADS, C_GRP, HEAD_DIM).transpose(0, 2, 1, 3).reshape(D_MODEL, _W_CQ)
    wt = jnp.concatenate([col("ak"), col("ik"), col("bq") * scale, col("bk"), col("bv"),
                          cq * scale, col("ck"), col("cv")], axis=1).astype(jnp.bfloat16)
    wT = jnp.concatenate([col("aq") * (scale * LOG2E), col("iq") * idx_scale, col("av"), col("iw")],
                         axis=1).T.astype(jnp.bfloat16)
    wg = col("gates").astype(jnp.bfloat16)
    gk = jnp.concatenate([jnp.ones((IDX_DIM,), jnp.float32), idx_k_norm])[None, :]
    wc = w_c.reshape(C_KV_HEADS, C_GRP, HEAD_DIM, D_MODEL).transpose(1, 0, 2, 3).reshape(_W_CQ, D_MODEL)
    return wt, wT, wg, gk, wc.astype(jnp.bfloat16)


def kernel(x, positions, attn_norm, w_in, idx_k_norm, sinks, w_a, w_b, w_c, w_o, mlp_norm, w_up, w_down, final_norm):
    B, L, D = x.shape
    depth = w_in.shape[0]
    assert D == D_MODEL and L % (16 * BLOCK) == 0
    tabs = _rope_tables(positions)
    bf = jnp.bfloat16
    h = x
    for l in range(depth):
        wt, wT, wg, gk, wc = _layer_weights(w_in[l], idx_k_norm[l], w_c[l])
        (ka, ik, bq0, bq1, bq2, bk0, bk1, bk2, bv0, bv1, bv2, cq, ck, cv, qa, iq, va, iw) = _inproj(
            h, attn_norm[l][None, :], wt, wT, tabs, gk)
        oa = _dsa(ik, ka, va, iq, qa, iw)
        obs, lses = [], []
        for q_, k_, v_ in ((bq0, bk0, bv0), (bq1, bk1, bv1), (bq2, bk2, bv2)):
            o, lse = _dilated_group(q_, k_, v_)
            obs.append(o)
            lses.append(lse)
        oc = _sinkwin(cq, ck, cv, sinks[l])
        h2 = _mixer_out(h, attn_norm[l][None, :], wg, oa, obs, lses, oc,
                        w_a[l].astype(bf), w_b[l].astype(bf), wc, w_o[l].astype(bf))
        h3 = _ffn(h2.reshape(B * L, D), mlp_norm[l][None, :], w_up[l].astype(bf), w_down[l].astype(bf),
                  final_norm[None, :], l == depth - 1)
        h = h3.reshape(B, L, D)
    return h
```
